```python
import jax, jax.numpy as jnp
from jax import lax
import numpy as np


D_MODEL = 2048
BATCH = 16
SEQ = 2048
DEPTH = 2

GRID_W = 64
GROUP_W = D_MODEL // 4
D_MIX = 4 * GROUP_W
CONV_K = 31
NA_HEADS = 8
NA_HEAD_DIM = GROUP_W // NA_HEADS
NA_ROWS = 8
NA_COLS = 16
GLA_HEADS = 4
GLA_DK = GROUP_W // 2 // GLA_HEADS
GLA_DV = GROUP_W // GLA_HEADS
GLA_RANK = 16
GLA_TAU = 16.0
GLA_CHUNK = 64
POOL_WINDOWS = (2, 4, 8, 16)
POOL_GROUPS = 4
POOL_CG = GROUP_W // POOL_GROUPS
EPS = 1e-6

IN_WIDTHS = (
    GROUP_W, GROUP_W, GROUP_W,
    GROUP_W, GROUP_W, GROUP_W, GROUP_W,
    GLA_HEADS * GLA_DK, GLA_HEADS * GLA_DK, GROUP_W, GROUP_W, GLA_RANK, GLA_RANK,
    GROUP_W, GROUP_W,
)
N_IN = sum(IN_WIDTHS)

kernel_name = 'hybrid_parallel_group_encoder_block'


def _rms_norm(x, g):
    xf = x.astype(jnp.float32)
    y = xf * lax.rsqrt(jnp.mean(xf * xf, axis=-1, keepdims=True) + EPS)
    return (y * g.astype(jnp.float32)).astype(x.dtype)


def _layer_norm(x, g, b):
    xf = x.astype(jnp.float32)
    mu = jnp.mean(xf, axis=-1, keepdims=True)
    var = jnp.mean(jnp.square(xf - mu), axis=-1, keepdims=True)
    y = (xf - mu) * lax.rsqrt(var + EPS)
    return (y * g.astype(jnp.float32) + b.astype(jnp.float32)).astype(x.dtype)


def _conformer_conv(u_val, u_glu, conv_w, conv_b, ln_g, ln_b):
    u = u_val * jax.nn.sigmoid(u_glu)
    y = lax.conv_general_dilated(
        u, conv_w[:, None, :].astype(u.dtype), window_strides=(1,),
        padding=[(CONV_K // 2, CONV_K // 2)],
        dimension_numbers=('NWC', 'WIO', 'NWC'),
        feature_group_count=u.shape[-1])
    y = y + conv_b.astype(y.dtype)
    return jax.nn.silu(_layer_norm(y, ln_g, ln_b))


def _neighbourhood_attention(q, k, v, rpb):
    B, S, H, Dh = q.shape
    rows = S // GRID_W
    kr = min(NA_ROWS, rows)
    kc = NA_COLS
    qg = jnp.moveaxis(q.reshape(B, rows, GRID_W, H, Dh), 1, 0)
    kg = k.reshape(B, rows, GRID_W, H, Dh)
    vg = v.reshape(B, rows, GRID_W, H, Dh)
    r_idx = jnp.arange(rows, dtype=jnp.int32)
    row_start = jnp.clip(r_idx - kr // 2, 0, rows - kr)
    c_idx = jnp.arange(GRID_W, dtype=jnp.int32)
    col_start = jnp.clip(c_idx - kc // 2, 0, GRID_W - kc)
    col_idx = col_start[:, None] + jnp.arange(kc, dtype=jnp.int32)[None, :]
    col_off = col_idx - c_idx[:, None] + (kc - 1)
    rpb_cols = rpb[:, :, col_off]
    scale = Dh ** -0.5

    def row_fn(inp):
        q_r, rs, r = inp
        k_rows = lax.dynamic_slice_in_dim(kg, rs, kr, axis=1)
        v_rows = lax.dynamic_slice_in_dim(vg, rs, kr, axis=1)
        k_win = k_rows[:, :, col_idx]
        v_win = v_rows[:, :, col_idx]
        row_off = rs + jnp.arange(kr, dtype=jnp.int32) - r + (NA_ROWS - 1)
        bias = jnp.transpose(rpb_cols[:, row_off], (0, 2, 1, 3))
        s = jnp.einsum('bqhd,brqchd->bhqrc', q_r, k_win).astype(jnp.float32) * scale
        s = s + bias.astype(jnp.float32)[None]
        p = jax.nn.softmax(s.reshape(B, H, GRID_W, kr * kc), axis=-1).reshape(s.shape)
        return jnp.einsum('bhqrc,brqchd->bqhd', p.astype(v_win.dtype), v_win)

    o = lax.map(row_fn, (qg, row_start, r_idx))
    return jnp.moveaxis(o, 0, 1).reshape(B, S, H, Dh)


def _gla_scan(q, k, v, g):
    B, S, H, dk = q.shape
    dv = v.shape[-1]
    C = GLA_CHUNK
    n = S // C

    def chunks(a):
        return jnp.moveaxis(a.reshape(B, n, C, H, a.shape[-1]), 1, 0)

    lower = jnp.tril(jnp.ones((C, C), jnp.float32))

    def step(state, inp):
        qc, kc, vc, gc = inp
        b = jnp.cumsum(gc, axis=1)
        b_last = b[:, -1]
        o_inter = jnp.einsum('bchk,bhkv->bchv', qc * jnp.exp(b), state)
        decay = jnp.exp(jnp.minimum(b[:, :, None] - b[:, None], 0.0)) * lower[None, :, :, None, None]
        attn = jnp.einsum('bihk,bjhk,bijhk->bhij', qc, kc, decay)
        o_intra = jnp.einsum('bhij,bjhv->bihv', attn, vc)
        state = state * jnp.exp(b_last)[..., None] + jnp.einsum(
            'bjhk,bjhv->bhkv', kc * jnp.exp(b_last[:, None] - b), vc)
        return state, o_inter + o_intra

    init = jnp.zeros((B, H, dk, dv), jnp.float32)
    _, o = lax.scan(step, init, (chunks(q), chunks(k), chunks(v), chunks(g)))
    return jnp.moveaxis(o, 0, 1).reshape(B, S, H, dv)


def _gla_branch(c_q, c_k, c_v, lr_f, lr_b, a2_f, ab_f, a2_b, ab_b, o_g):
    B, S, _ = c_q.shape
    f32 = jnp.float32
    q = c_q.astype(f32).reshape(B, S, GLA_HEADS, GLA_DK) * (GLA_DK ** -0.5)
    k = c_k.astype(f32).reshape(B, S, GLA_HEADS, GLA_DK)
    v = c_v.astype(f32).reshape(B, S, GLA_HEADS, GLA_DV)

    def log_decay(lr, a2, ab):
        z = jnp.einsum('bsr,rk->bsk', lr.astype(f32), a2.astype(f32)) + ab.astype(f32)
        return (jax.nn.log_sigmoid(z) / GLA_TAU).reshape(B, S, GLA_HEADS, GLA_DK)

    g_f = log_decay(lr_f, a2_f, ab_f)
    g_b = log_decay(lr_b, a2_b, ab_b)
    flip = lambda a: jnp.flip(a, axis=1)
    o = _gla_scan(q, k, v, g_f) + flip(_gla_scan(flip(q), flip(k), flip(v), flip(g_b)))
    o = o * lax.rsqrt(jnp.mean(o * o, axis=-1, keepdims=True) + EPS) * o_g.astype(f32)
    return o.reshape(B, S, GROUP_W)


def _multiscale_pool(u, w, scale):
    B, S, _ = u.shape
    f32 = jnp.float32
    uf = u.astype(f32).reshape(B, S, POOL_GROUPS, POOL_CG)
    cs = jnp.concatenate([jnp.zeros((B, 1, POOL_GROUPS, POOL_CG), f32), jnp.cumsum(uf, axis=1)], axis=1)
    win = jnp.array(POOL_WINDOWS, jnp.int32)[None, :]
    t = jnp.arange(S, dtype=jnp.int32)[:, None]
    lo = jnp.clip(t - win // 2, 0, S)
    hi = jnp.clip(t + win - win // 2, 0, S)
    g_idx = jnp.arange(POOL_GROUPS, dtype=jnp.int32)[None, :]
    mean = (cs[:, hi, g_idx] - cs[:, lo, g_idx]) / (hi - lo).astype(f32)[None, :, :, None]
    y = jnp.einsum('bsgc,gcd->bsgd', mean - uf, w.astype(f32))
    return y.reshape(B, S, GROUP_W) * scale.astype(f32)


def _hybrid_layer(x, norm_g, w_in, conv_w, conv_b, conv_ln_g, conv_ln_b, na_q_g, na_k_g, na_rpb,
                  gla_a2_f, gla_ab_f, gla_a2_b, gla_ab_b, gla_o_g, pool_w, pool_scale, w_out):
    B, S, _ = x.shape
    h = _rms_norm(x, norm_g)
    z = jnp.einsum('bsd,dn->bsn', h, w_in.astype(h.dtype))
    offsets = np.cumsum(IN_WIDTHS)[:-1].tolist()
    (a_val, a_glu, a_gate, b_q, b_k, b_v, b_gate,
     c_q, c_k, c_v, c_gate, c_lr_f, c_lr_b, d_val, d_gate) = jnp.split(z, offsets, axis=-1)
    dt = h.dtype
    y_a = _conformer_conv(a_val, a_glu, conv_w, conv_b, conv_ln_g, conv_ln_b) * jax.nn.silu(a_gate)
    q = _rms_norm(b_q.reshape(B, S, NA_HEADS, NA_HEAD_DIM), na_q_g)
    k = _rms_norm(b_k.reshape(B, S, NA_HEADS, NA_HEAD_DIM), na_k_g)
    v = b_v.reshape(B, S, NA_HEADS, NA_HEAD_DIM)
    y_b = _neighbourhood_attention(q, k, v, na_rpb).reshape(B, S, GROUP_W) * jax.nn.silu(b_gate)
    y_c = _gla_branch(c_q, c_k, c_v, c_lr_f, c_lr_b, gla_a2_f, gla_ab_f, gla_a2_b, gla_ab_b,
                      gla_o_g).astype(dt) * jax.nn.silu(c_gate)
    y_d = _multiscale_pool(d_val, pool_w, pool_scale).astype(dt) * jax.nn.silu(d_gate)
    y = jnp.concatenate([y_a.astype(dt), y_b.astype(dt), y_c.astype(dt), y_d.astype(dt)], axis=-1)
    return x + jnp.einsum('bsm,md->bsd', y, w_out.astype(dt)).astype(x.dtype)


def setup_inputs(seed: int = 0) -> dict:
    key = jax.random.key(seed)
    ks = jax.random.split(key, 18)
    L = DEPTH
    n = lambda k, s: jax.random.normal(k, s, jnp.float32)
    return {
        'x': n(ks[0], (BATCH, SEQ, D_MODEL)),
        'norm_g': 1.0 + 0.02 * n(ks[1], (L, D_MODEL)),
        'w_in': n(ks[2], (L, D_MODEL, N_IN)) * D_MODEL ** -0.5,
        'conv_w': n(ks[3], (L, CONV_K, GROUP_W)) * CONV_K ** -0.5,
        'conv_b': 0.02 * n(ks[4], (L, GROUP_W)),
        'conv_ln_g': 1.0 + 0.02 * n(ks[5], (L, GROUP_W)),
        'conv_ln_b': 0.02 * n(ks[6], (L, GROUP_W)),
        'na_q_g': 1.0 + 0.02 * n(ks[7], (L, NA_HEADS, NA_HEAD_DIM)),
        'na_k_g': 1.0 + 0.02 * n(ks[8], (L, NA_HEADS, NA_HEAD_DIM)),
        'na_rpb': 0.1 * n(ks[9], (L, NA_HEADS, 2 * NA_ROWS - 1, 2 * NA_COLS - 1)),
        'gla_a2_f': n(ks[10], (L, GLA_RANK, GLA_HEADS * GLA_DK)) * GLA_RANK ** -0.5,
        'gla_ab_f': 1.0 + 0.5 * n(ks[11], (L, GLA_HEADS * GLA_DK)),
        'gla_a2_b': n(ks[12], (L, GLA_RANK, GLA_HEADS * GLA_DK)) * GLA_RANK ** -0.5,
        'gla_ab_b': 1.0 + 0.5 * n(ks[13], (L, GLA_HEADS * GLA_DK)),
        'gla_o_g': 1.0 + 0.02 * n(ks[14], (L, GLA_HEADS, GLA_DV)),
        'pool_w': n(ks[15], (L, POOL_GROUPS, POOL_CG, POOL_CG)) * POOL_CG ** -0.5,
        'pool_scale': 1.0 + 0.02 * n(ks[16], (L, GROUP_W)),
        'w_out': n(ks[17], (L, D_MIX, D_MODEL)) * D_MIX ** -0.5,
    }


def reference(x, norm_g, w_in, conv_w, conv_b, conv_ln_g, conv_ln_b, na_q_g, na_k_g, na_rpb,
              gla_a2_f, gla_ab_f, gla_a2_b, gla_ab_b, gla_o_g, pool_w, pool_scale, w_out):
    for l in range(DEPTH):
        x = _hybrid_layer(x, norm_g[l], w_in[l], conv_w[l], conv_b[l], conv_ln_g[l], conv_ln_b[l],
                          na_q_g[l], na_k_g[l], na_rpb[l], gla_a2_f[l], gla_ab_f[l], gla_a2_b[l],
                          gla_ab_b[l], gla_o_g[l], pool_w[l], pool_scale[l], w_out[l])
    return x
```

```python
import functools

import jax
import jax.numpy as jnp
from jax import lax
from jax.experimental import pallas as pl
from jax.experimental.pallas import tpu as pltpu

F32 = jnp.float32
BF16 = jnp.bfloat16

EPS = 1e-6
GRID_W = 64
GROUP_W = 512
CONV_K = 31
NA_HEADS = 8
NA_HEAD_DIM = 64
NA_ROWS = 8
NA_COLS = 16
GLA_HEADS = 4
GLA_DK = 64
GLA_DV = 128
GLA_RANK = 16
GLA_TAU = 16.0
GLA_CHUNK = 64
POOL_WINDOWS = (2, 4, 8, 16)
POOL_CG = 128
LANES = 128
SUBLANES = 8
NEG_BIAS = -1e30

VMEM_LIMIT = 56 * 1024 * 1024
INPROJ_TILES = dict(tm=1024, tn=1536)
OUTPROJ_TILES = dict(tm=512)


def _cparams(*sem):
    return pltpu.CompilerParams(dimension_semantics=sem, vmem_limit_bytes=VMEM_LIMIT)


def _silu(x):
    return x * jax.nn.sigmoid(x)


def _dot(a, b):
    return jnp.dot(a, b, preferred_element_type=F32)


def _dot_nt(a, b):
    return lax.dot_general(a, b, (((1,), (1,)), ((), ())), preferred_element_type=F32)


def _dot_tn(a, b):
    return lax.dot_general(a, b, (((0,), (0,)), ((), ())), preferred_element_type=F32)


def _split3(x):
    hi = x.astype(BF16)
    r1 = x - hi.astype(F32)
    mid = r1.astype(BF16)
    lo = (r1 - mid.astype(F32)).astype(BF16)
    return hi, mid, lo


def _inproj_kernel(x_ref, g_ref, w_ref, wlr_ref, z_ref, lr_ref, h_ref, *, rows_per_step):
    j = pl.program_id(1)
    tm = x_ref.shape[0]

    @pl.when(j == 0)
    def _():
        def body(i, c):
            r = pl.ds(pl.multiple_of(i * rows_per_step, rows_per_step), rows_per_step)
            x = x_ref[r, :]
            ms = jnp.mean(x * x, axis=-1, keepdims=True)
            h_ref[r, :] = (x * lax.rsqrt(ms + EPS) * g_ref[...]).astype(BF16)
            return c
        lax.fori_loop(0, tm // rows_per_step, body, 0)
        lr_ref[...] = _dot(h_ref[...], wlr_ref[...])

    z_ref[...] = _dot(h_ref[...], w_ref[...]).astype(BF16)


def _inproj(x2d, norm_g, w_main, w_lr, *, tm, tn):
    m, d = x2d.shape
    n = w_main.shape[1]
    assert m % tm == 0 and n % tn == 0
    return pl.pallas_call(
        functools.partial(_inproj_kernel, rows_per_step=64),
        grid=(m // tm, n // tn),
        in_specs=[
            pl.BlockSpec((tm, d), lambda i, j: (i, 0)),
            pl.BlockSpec((1, d), lambda i, j: (0, 0)),
            pl.BlockSpec((d, tn), lambda i, j: (0, j)),
            pl.BlockSpec((d, LANES), lambda i, j: (0, 0)),
        ],
        out_specs=[
            pl.BlockSpec((tm, tn), lambda i, j: (i, j)),
            pl.BlockSpec((tm, LANES), lambda i, j: (i, 0)),
        ],
        out_shape=[
            jax.ShapeDtypeStruct((m, n), BF16),
            jax.ShapeDtypeStruct((m, LANES), F32),
        ],
        scratch_shapes=[pltpu.VMEM((tm, d), BF16)],
        compiler_params=_cparams("parallel", "arbitrary"),
        name="inproj",
    )(x2d, norm_g, w_main, w_lr)


CONV_PAD = 16


def _conv_kernel(val_ref, glu_ref, gate_ref, cw_ref, cb_ref, lg_ref, lb_ref, out_ref, u_ref):
    s = val_ref.shape[0]
    w = val_ref.shape[1]
    fill_rows = 128
    tile = 32

    u_ref[0:CONV_PAD, :] = jnp.zeros((CONV_PAD, w), F32)
    u_ref[CONV_PAD + s:CONV_PAD + s + CONV_PAD, :] = jnp.zeros((CONV_PAD, w), F32)

    def fill(i, c):
        base = pl.multiple_of(i * fill_rows, fill_rows)
        v = val_ref[pl.ds(base, fill_rows), :].astype(F32)
        g = glu_ref[pl.ds(base, fill_rows), :].astype(F32)
        u_ref[pl.ds(base + CONV_PAD, fill_rows), :] = v * jax.nn.sigmoid(g)
        return c
    lax.fori_loop(0, s // fill_rows, fill, 0)

    win_rows = tile + 2 * CONV_PAD

    def conv(i, c):
        base = pl.multiple_of(i * tile, tile)
        acc = jnp.zeros((tile, w), F32)
        for r in range(SUBLANES):
            offs = [o for o in range(CONV_PAD - CONV_K // 2, CONV_PAD + CONV_K // 2 + 1) if o % SUBLANES == r]
            win = u_ref[pl.ds(base, win_rows), :]
            if r:
                win = pltpu.roll(win, win_rows - r, axis=0)
            for o in offs:
                k = o - (CONV_PAD - CONV_K // 2)
                acc = acc + win[o - r:o - r + tile, :] * cw_ref[k:k + 1, :]
        y = acc + cb_ref[...]
        mu = jnp.mean(y, axis=-1, keepdims=True)
        d = y - mu
        var = jnp.mean(d * d, axis=-1, keepdims=True)
        yn = d * lax.rsqrt(var + EPS) * lg_ref[...] + lb_ref[...]
        gate = gate_ref[pl.ds(base, tile), :].astype(F32)
        out_ref[pl.ds(base, tile), :] = (_silu(yn) * _silu(gate)).astype(BF16)
        return c
    lax.fori_loop(0, s // tile, conv, 0)


def _conv_mixer(z3, conv_w, conv_b, ln_g, ln_b):
    b, s, _ = z3.shape
    w = GROUP_W
    col = lambda j: pl.BlockSpec((None, s, w), lambda i, j=j: (i, 0, j))
    vec = lambda r: pl.BlockSpec((r, w), lambda i: (0, 0))
    return pl.pallas_call(
        _conv_kernel,
        grid=(b,),
        in_specs=[col(0), col(1), col(2), vec(CONV_K), vec(1), vec(1), vec(1)],
        out_specs=pl.BlockSpec((None, s, w), lambda i: (i, 0, 0)),
        out_shape=jax.ShapeDtypeStruct((b, s, w), BF16),
        scratch_shapes=[pltpu.VMEM((s + 2 * CONV_PAD, w), F32)],
        compiler_params=_cparams("parallel"),
        name="conv_mixer",
    )(z3, z3, z3, conv_w, conv_b, ln_g, ln_b)


POOL_PAD = 8


def _pool_kernel(val_ref, gate_ref, pw_ref, ps_ref, out_ref, u_ref):
    s = val_ref.shape[0]
    w = val_ref.shape[1]
    tile = 128

    u_ref[0:POOL_PAD, :] = jnp.zeros((POOL_PAD, w), F32)
    u_ref[POOL_PAD + s:POOL_PAD + s + POOL_PAD, :] = jnp.zeros((POOL_PAD, w), F32)

    def fill(i, c):
        base = pl.multiple_of(i * tile, tile)
        u_ref[pl.ds(base + POOL_PAD, tile), :] = val_ref[pl.ds(base, tile), :].astype(F32)
        return c
    lax.fori_loop(0, s // tile, fill, 0)

    win_rows = tile + 2 * POOL_PAD

    def shifted(a, o):
        return a if o == 0 else pltpu.roll(a, win_rows - o, axis=0)

    def pool(i, c):
        base = pl.multiple_of(i * tile, tile)
        t = base + lax.broadcasted_iota(jnp.int32, (tile, 1), 0)
        ys = []
        for g, win in enumerate(POOL_WINDOWS):
            half = win // 2
            lanes = slice(g * POOL_CG, (g + 1) * POOL_CG)
            x = u_ref[pl.ds(base, win_rows), lanes]
            sm, span = x, 1
            while span < win:
                sm = sm + shifted(sm, span)
                span *= 2
            acc = shifted(sm, POOL_PAD - half)[0:tile]
            cnt = jnp.minimum(t + half, s) - jnp.maximum(t - half, 0)
            centre = x[POOL_PAD:POOL_PAD + tile]
            diff = acc / cnt.astype(F32) - centre
            ys.append(_dot(diff.astype(BF16), pw_ref[g]))
        y = jnp.concatenate(ys, axis=-1) * ps_ref[...]
        gate = gate_ref[pl.ds(base, tile), :].astype(F32)
        out_ref[pl.ds(base, tile), :] = (y * _silu(gate)).astype(BF16)
        return c
    lax.fori_loop(0, s // tile, pool, 0)


def _pool_mixer(z3, pool_w, pool_scale, *, first_block):
    b, s, _ = z3.shape
    w = GROUP_W
    col = lambda j: pl.BlockSpec((None, s, w), lambda i, j=j: (i, 0, j))
    return pl.pallas_call(
        _pool_kernel,
        grid=(b,),
        in_specs=[
            col(first_block), col(first_block + 1),
            pl.BlockSpec(pool_w.shape, lambda i: (0, 0, 0)),
            pl.BlockSpec((1, w), lambda i: (0, 0)),
        ],
        out_specs=pl.BlockSpec((None, s, w), lambda i: (i, 0, 0)),
        out_shape=jax.ShapeDtypeStruct((b, s, w), BF16),
        scratch_shapes=[pltpu.VMEM((s + 2 * POOL_PAD, w), F32)],
        compiler_params=_cparams("parallel"),
        name="pool_mixer",
    )(z3, z3, pool_w, pool_scale)


def _head_rms(x, ones_bd, gain):
    ss = _dot((x * x).astype(BF16), ones_bd)
    return x * lax.rsqrt(ss * (1.0 / NA_HEAD_DIM) + EPS) * gain


def _na_kernel(q_ref, k_ref, v_ref, gate_ref, bias_ref, qg_ref, kg_ref, ones_ref, out_ref,
               kn_ref, vlo_ref, vhi_ref, *, n_rows):
    r = pl.program_id(1)
    s = k_ref.shape[0]
    w = k_ref.shape[1]
    n_keys = NA_ROWS * GRID_W
    lane = lax.broadcasted_iota(jnp.int32, (1, w), 1)
    low_half = (lane % LANES) < NA_HEAD_DIM

    @pl.when(r == 0)
    def _():
        rows = 256

        def prep(i, c):
            sl = pl.ds(pl.multiple_of(i * rows, rows), rows)
            k = k_ref[sl, :].astype(F32)
            kn_ref[sl, :] = _head_rms(k, ones_ref[...], kg_ref[...]).astype(BF16)
            v = v_ref[sl, :]
            zero = jnp.zeros_like(v)
            vlo_ref[sl, :] = jnp.where(low_half, v, zero)
            vhi_ref[sl, :] = jnp.where(low_half, zero, v)
            return c
        lax.fori_loop(0, s // rows, prep, 0)

    q = q_ref[...].astype(F32)
    qn = (_head_rms(q, ones_ref[...], qg_ref[...]) * (NA_HEAD_DIM ** -0.5)).astype(BF16)
    row_start = jnp.clip(r - NA_ROWS // 2, 0, n_rows - NA_ROWS)
    keys = pl.ds(pl.multiple_of(row_start * GRID_W, GRID_W), n_keys)
    lane_p = lax.broadcasted_iota(jnp.int32, (1, LANES), 1)
    low_p = lane_p < NA_HEAD_DIM
    gate = gate_ref[...].astype(F32)

    for p in range(NA_HEADS // 2):
        lanes = slice(p * LANES, (p + 1) * LANES)
        qp = qn[:, lanes]
        zero = jnp.zeros_like(qp)
        qm = jnp.concatenate([jnp.where(low_p, qp, zero), jnp.where(low_p, zero, qp)], axis=0)
        sc = _dot_nt(qm, kn_ref[keys, lanes]) + bias_ref[p]
        m = jnp.max(sc, axis=-1, keepdims=True)
        e = jnp.exp(sc - m)
        inv = 1.0 / jnp.sum(e, axis=-1, keepdims=True)
        eb = e.astype(BF16)
        o = (_dot(eb[0:GRID_W], vlo_ref[keys, lanes]) + _dot(eb[GRID_W:], vhi_ref[keys, lanes]))
        o = o * jnp.where(low_p, inv[0:GRID_W], inv[GRID_W:])
        out_ref[:, lanes] = (o * _silu(gate[:, lanes])).astype(BF16)


def _na_bias_table(rpb):
    c = jnp.arange(GRID_W, dtype=jnp.int32)
    cs = jnp.clip(c - NA_COLS // 2, 0, GRID_W - NA_COLS)
    cp = jnp.arange(GRID_W, dtype=jnp.int32)
    valid = (cp[None, :] >= cs[:, None]) & (cp[None, :] < cs[:, None] + NA_COLS)
    col_off = jnp.clip(cp[None, :] - c[:, None] + NA_COLS - 1, 0, 2 * NA_COLS - 2)
    di = jnp.arange(NA_ROWS, dtype=jnp.int32)
    jr = jnp.arange(NA_ROWS, dtype=jnp.int32)
    row_off = jr[None, :] + (NA_ROWS - 1) - di[:, None]
    t = rpb[:, row_off[:, :, None, None], col_off[None, None, :, :]]
    t = jnp.where(valid[None, None, None], t.astype(F32), NEG_BIAS)
    t = jnp.transpose(t, (1, 0, 3, 2, 4))
    return t.reshape(NA_ROWS, NA_HEADS // 2, 2 * GRID_W, NA_ROWS * GRID_W)


def _na_mixer(z3, bias_tbl, q_g, k_g, ones_bd, *, first_block):
    b, s, _ = z3.shape
    w = GROUP_W
    n_rows = s // GRID_W
    assert n_rows >= NA_ROWS
    fb = first_block

    def bias_index(i, r):
        return (r - jnp.clip(r - NA_ROWS // 2, 0, n_rows - NA_ROWS), 0, 0, 0)

    return pl.pallas_call(
        functools.partial(_na_kernel, n_rows=n_rows),
        grid=(b, n_rows),
        in_specs=[
            pl.BlockSpec((None, GRID_W, w), lambda i, r: (i, r, fb)),
            pl.BlockSpec((None, s, w), lambda i, r: (i, 0, fb + 1)),
            pl.BlockSpec((None, s, w), lambda i, r: (i, 0, fb + 2)),
            pl.BlockSpec((None, GRID_W, w), lambda i, r: (i, r, fb + 3)),
            pl.BlockSpec((None,) + bias_tbl.shape[1:], bias_index),
            pl.BlockSpec((1, w), lambda i, r: (0, 0)),
            pl.BlockSpec((1, w), lambda i, r: (0, 0)),
            pl.BlockSpec((w, w), lambda i, r: (0, 0)),
        ],
        out_specs=pl.BlockSpec((None, GRID_W, w), lambda i, r: (i, r, 0)),
        out_shape=jax.ShapeDtypeStruct((b, s, w), BF16),
        scratch_shapes=[pltpu.VMEM((s, w), BF16)] * 3,
        compiler_params=_cparams("parallel", "arbitrary"),
        name="na_mixer",
    )(z3, z3, z3, z3, bias_tbl, q_g, k_g, ones_bd)


def _gla_kernel(q_ref, k_ref, v_ref, gate_ref, lr_ref, a2f_ref, abf_ref, a2b_ref, abb_ref,
                tri_ref, bd_ref, og_ref, out_ref,
                bf_ref, bb_ref, of_ref, ob_ref, stf_ref, stb_ref):
    s = q_ref.shape[0]
    dk_all = q_ref.shape[1]
    dv_all = v_ref.shape[1]
    c = GLA_CHUNK
    n_chunks = s // c
    rows = tri_ref.shape[1]

    def decay(i, carry):
        sl = pl.ds(pl.multiple_of(i * rows, rows), rows)
        lr = lr_ref[sl, :]
        lr_hi = lr.astype(BF16)
        lr_lo = (lr - lr_hi.astype(F32)).astype(BF16)
        for a2_ref, ab_ref, b_ref, tri in ((a2f_ref, abf_ref, bf_ref, tri_ref[0]),
                                          (a2b_ref, abb_ref, bb_ref, tri_ref[1])):
            z = (_dot(lr_hi, a2_ref[0]) + _dot(lr_lo, a2_ref[0]) + _dot(lr_hi, a2_ref[1])) + ab_ref[...]
            g = (jnp.minimum(z, 0.0) - jnp.log1p(jnp.exp(-jnp.abs(z)))) * (1.0 / GLA_TAU)
            g_hi, g_mid, g_lo = _split3(g)
            b_ref[sl, :] = _dot(tri, g_hi) + _dot(tri, g_mid) + _dot(tri, g_lo)
        return carry
    lax.fori_loop(0, s // rows, decay, 0)

    stf_ref[...] = jnp.zeros_like(stf_ref)
    stb_ref[...] = jnp.zeros_like(stb_ref)

    lane = lax.broadcasted_iota(jnp.int32, (1, dk_all), 1)
    head_masks = [(lane // GLA_DK) == h for h in range(GLA_HEADS)]
    ri = lax.broadcasted_iota(jnp.int32, (c, c), 0)
    ci = lax.broadcasted_iota(jnp.int32, (c, c), 1)
    causal = (ri >= ci, ri <= ci)

    def one_chunk(ch, forward):
        b_ref, st_ref, o_ref = (bf_ref, stf_ref, of_ref) if forward else (bb_ref, stb_ref, ob_ref)
        sl = pl.ds(pl.multiple_of(ch * c, c), c)
        b = b_ref[sl, :]
        b_last = b[c - 1:c, :] if forward else b[0:1, :]
        q = q_ref[sl, :].astype(F32)
        k = k_ref[sl, :].astype(F32)
        v = v_ref[sl, :]
        qd = (q * ((GLA_DK ** -0.5) * jnp.exp(b))).astype(BF16)
        ki = (k * jnp.exp(-b)).astype(BF16)
        kd = (k * jnp.exp(b_last - b)).astype(BF16)
        zero = jnp.zeros_like(qd)
        q_stack = jnp.concatenate([jnp.where(hm, qd, zero) for hm in head_masks], axis=0)
        attn = _dot_nt(q_stack, ki)
        mask = causal[0] if forward else causal[1]
        state = st_ref[...]
        o = _dot_nt(qd, state.astype(BF16))
        intra = []
        for h in range(GLA_HEADS):
            a_h = jnp.where(mask, attn[h * c:(h + 1) * c, :], 0.0).astype(BF16)
            intra.append(_dot(a_h, v[:, h * GLA_DV:(h + 1) * GLA_DV]))
        o_ref[sl, :] = o + jnp.concatenate(intra, axis=-1)
        st_ref[...] = state * jnp.exp(b_last) + _dot_tn(v, kd) * bd_ref[...]

    def step(t, carry):
        one_chunk(t, True)
        one_chunk(n_chunks - 1 - t, False)
        return carry
    lax.fori_loop(0, n_chunks, step, 0)

    def finish(i, carry):
        sl = pl.ds(pl.multiple_of(i * rows, rows), rows)
        o = of_ref[sl, :] + ob_ref[sl, :]
        outs = []
        for h in range(GLA_HEADS):
            oh = o[:, h * GLA_DV:(h + 1) * GLA_DV]
            ms = jnp.mean(oh * oh, axis=-1, keepdims=True)
            outs.append(oh * lax.rsqrt(ms + EPS))
        y = jnp.concatenate(outs, axis=-1) * og_ref[...]
        gate = gate_ref[sl, :].astype(F32)
        out_ref[sl, :] = (y * _silu(gate)).astype(BF16)
        return carry
    lax.fori_loop(0, s // rows, finish, 0)


def _gla_mixer(z3, lr3, a2f, abf, a2b, abb, tri, bd_mask, o_g, *, q_block, v_block):
    b, s, _ = z3.shape
    dk_all = GLA_HEADS * GLA_DK
    dv_all = GLA_HEADS * GLA_DV
    full = lambda a: pl.BlockSpec(a.shape, lambda i, n=a.ndim: (0,) * n)
    return pl.pallas_call(
        _gla_kernel,
        grid=(b,),
        in_specs=[
            pl.BlockSpec((None, s, dk_all), lambda i: (i, 0, q_block)),
            pl.BlockSpec((None, s, dk_all), lambda i: (i, 0, q_block + 1)),
            pl.BlockSpec((None, s, dv_all), lambda i: (i, 0, v_block)),
            pl.BlockSpec((None, s, dv_all), lambda i: (i, 0, v_block + 1)),
            pl.BlockSpec((None, s, LANES), lambda i: (i, 0, 0)),
            full(a2f), full(abf), full(a2b), full(abb), full(tri), full(bd_mask), full(o_g),
        ],
        out_specs=pl.BlockSpec((None, s, dv_all), lambda i: (i, 0, 0)),
        out_shape=jax.ShapeDtypeStruct((b, s, dv_all), BF16),
        scratch_shapes=[
            pltpu.VMEM((s, dk_all), F32), pltpu.VMEM((s, dk_all), F32),
            pltpu.VMEM((s, dv_all), F32), pltpu.VMEM((s, dv_all), F32),
            pltpu.VMEM((dv_all, dk_all), F32), pltpu.VMEM((dv_all, dk_all), F32),
        ],
        compiler_params=_cparams("parallel"),
        name="gla_mixer",
    )(z3, z3, z3, z3, lr3, a2f, abf, a2b, abb, tri, bd_mask, o_g)


def _outproj_kernel(x_ref, ya_ref, yb_ref, yc_ref, yd_ref, w_ref, o_ref):
    y = jnp.concatenate([ya_ref[...], yb_ref[...], yc_ref[...], yd_ref[...]], axis=-1)
    o_ref[...] = x_ref[...] + _dot(y, w_ref[...])


def _outproj(x2d, ys, w_out, *, tm):
    m, d = x2d.shape
    gw = ys[0].shape[1]
    yspec = pl.BlockSpec((tm, gw), lambda i: (i, 0))
    return pl.pallas_call(
        _outproj_kernel,
        grid=(m // tm,),
        in_specs=[pl.BlockSpec((tm, d), lambda i: (i, 0)), yspec, yspec, yspec, yspec,
                  pl.BlockSpec(w_out.shape, lambda i: (0, 0))],
        out_specs=pl.BlockSpec((tm, d), lambda i: (i, 0)),
        out_shape=jax.ShapeDtypeStruct((m, d), F32),
        compiler_params=_cparams("parallel"),
        name="outproj",
    )(x2d, *ys, w_out)


def _gla_constants():
    rows = 4 * GLA_CHUNK
    i = jnp.arange(rows)
    same = (i[:, None] // GLA_CHUNK) == (i[None, :] // GLA_CHUNK)
    lower = same & (i[:, None] >= i[None, :])
    upper = same & (i[:, None] <= i[None, :])
    tri = jnp.stack([lower, upper]).astype(BF16)
    rv = jnp.arange(GLA_HEADS * GLA_DV) // GLA_DV
    rk = jnp.arange(GLA_HEADS * GLA_DK) // GLA_DK
    bd_mask = (rv[:, None] == rk[None, :]).astype(F32)
    return tri, bd_mask


def _pad_rank(a2, first_row):
    full = jnp.zeros((LANES, a2.shape[1]), F32).at[first_row:first_row + a2.shape[0]].set(a2.astype(F32))
    hi = full.astype(BF16)
    lo = (full - hi.astype(F32)).astype(BF16)
    return jnp.stack([hi, lo])


def _layer(x, norm_g, w_in, conv_w, conv_b, conv_ln_g, conv_ln_b, na_q_g, na_k_g, na_rpb,
           gla_a2_f, gla_ab_f, gla_a2_b, gla_ab_b, gla_o_g, pool_w, pool_scale, w_out, consts):
    b, s, d = x.shape
    m = b * s
    tri, bd_mask, ones_bd = consts
    lr0 = 7 * GROUP_W + 2 * GLA_HEADS * GLA_DK + 2 * GROUP_W
    lr1 = lr0 + 2 * GLA_RANK
    w_main = jnp.concatenate([w_in[:, :lr0], w_in[:, lr1:]], axis=1).astype(BF16)
    w_lr = jnp.zeros((d, LANES), F32).at[:, :2 * GLA_RANK].set(w_in[:, lr0:lr1]).astype(BF16)

    x2d = x.reshape(m, d)
    z, lr = _inproj(x2d, norm_g.reshape(1, d), w_main, w_lr, **INPROJ_TILES)
    z3 = z.reshape(b, s, z.shape[1])
    lr3 = lr.reshape(b, s, LANES)

    row = lambda a: a.reshape(1, -1).astype(F32)
    y_a = _conv_mixer(z3, conv_w.astype(F32), row(conv_b), row(conv_ln_g), row(conv_ln_b))
    y_b = _na_mixer(z3, _na_bias_table(na_rpb), row(na_q_g), row(na_k_g), ones_bd, first_block=3)
    y_c = _gla_mixer(z3, lr3, _pad_rank(gla_a2_f, 0), row(gla_ab_f), _pad_rank(gla_a2_b, GLA_RANK),
                     row(gla_ab_b), tri, bd_mask, row(gla_o_g), q_block=14, v_block=8)
    y_d = _pool_mixer(z3, pool_w.astype(BF16), row(pool_scale), first_block=10)

    ys = [y.reshape(m, GROUP_W) for y in (y_a, y_b, y_c, y_d)]
    out = _outproj(x2d, ys, w_out.astype(BF16), **OUTPROJ_TILES)
    return out.reshape(b, s, d)


def kernel(x, norm_g, w_in, conv_w, conv_b, conv_ln_g, conv_ln_b, na_q_g, na_k_g, na_rpb, gla_a2_f,
           gla_ab_f, gla_a2_b, gla_ab_b, gla_o_g, pool_w, pool_scale, w_out):
    tri, bd_mask = _gla_constants()
    hq = jnp.arange(GROUP_W) // NA_HEAD_DIM
    ones_bd = (hq[:, None] == hq[None, :]).astype(BF16)
    consts = (tri, bd_mask, ones_bd)
    params = (norm_g, w_in, conv_w, conv_b, conv_ln_g, conv_ln_b, na_q_g, na_k_g, na_rpb, gla_a2_f,
              gla_ab_f, gla_a2_b, gla_ab_b, gla_o_g, pool_w, pool_scale, w_out)
    for l in range(norm_g.shape[0]):
        x = _layer(x, *[p[l] for p in params], consts)
    return x
```

```python
import functools

import jax
import jax.numpy as jnp
from jax import lax
from jax.experimental import pallas as pl
from jax.experimental.pallas import tpu as pltpu

F32 = jnp.float32
BF16 = jnp.bfloat16

EPS = 1e-6
GRID_W = 64
GROUP_W = 512
CONV_K = 31
NA_HEADS = 8
NA_HEAD_DIM = 64
NA_ROWS = 8
NA_COLS = 16
GLA_HEADS = 4
GLA_DK = 64
GLA_DV = 128
GLA_RANK = 16
GLA_TAU = 16.0
GLA_CHUNK = 64
POOL_WINDOWS = (2, 4, 8, 16)
POOL_CG = 128
LANES = 128
SUBLANES = 8
NEG_BIAS = -1e30
NA_ROWS_PER_STEP = 4

VMEM_LIMIT = 56 * 1024 * 1024
INPROJ_TILES = dict(tm=1024, tn=1536)
OUTPROJ_TILES = dict(tm=512)


def _cparams(*sem):
    return pltpu.CompilerParams(dimension_semantics=sem, vmem_limit_bytes=VMEM_LIMIT)


def _silu(x):
    return x * jax.nn.sigmoid(x)


def _dot(a, b):
    return jnp.dot(a, b, preferred_element_type=F32)


def _dot_nt(a, b):
    return lax.dot_general(a, b, (((1,), (1,)), ((), ())), preferred_element_type=F32)


def _dot_tn(a, b):
    return lax.dot_general(a, b, (((0,), (0,)), ((), ())), preferred_element_type=F32)


def _split3(x):
    hi = x.astype(BF16)
    r1 = x - hi.astype(F32)
    mid = r1.astype(BF16)
    lo = (r1 - mid.astype(F32)).astype(BF16)
    return hi, mid, lo


def _inproj_kernel(x_ref, g_ref, w_ref, wlr_ref, z_ref, lr_ref, h_ref, *, rows_per_step):
    j = pl.program_id(1)
    tm = x_ref.shape[0]

    @pl.when(j == 0)
    def _():
        def body(i, c):
            r = pl.ds(pl.multiple_of(i * rows_per_step, rows_per_step), rows_per_step)
            x = x_ref[r, :]
            ms = jnp.mean(x * x, axis=-1, keepdims=True)
            h_ref[r, :] = (x * lax.rsqrt(ms + EPS) * g_ref[...]).astype(BF16)
            return c
        lax.fori_loop(0, tm // rows_per_step, body, 0)
        lr_ref[...] = _dot(h_ref[...], wlr_ref[...])

    z_ref[...] = _dot(h_ref[...], w_ref[...]).astype(BF16)


def _inproj(x2d, norm_g, w_main, w_lr, *, tm, tn):
    m, d = x2d.shape
    n = w_main.shape[1]
    assert m % tm == 0 and n % tn == 0
    return pl.pallas_call(
        functools.partial(_inproj_kernel, rows_per_step=64),
        grid=(m // tm, n // tn),
        in_specs=[
            pl.BlockSpec((tm, d), lambda i, j: (i, 0)),
            pl.BlockSpec((1, d), lambda i, j: (0, 0)),
            pl.BlockSpec((d, tn), lambda i, j: (0, j)),
            pl.BlockSpec((d, LANES), lambda i, j: (0, 0)),
        ],
        out_specs=[
            pl.BlockSpec((tm, tn), lambda i, j: (i, j)),
            pl.BlockSpec((tm, LANES), lambda i, j: (i, 0)),
        ],
        out_shape=[
            jax.ShapeDtypeStruct((m, n), BF16),
            jax.ShapeDtypeStruct((m, LANES), F32),
        ],
        scratch_shapes=[pltpu.VMEM((tm, d), BF16)],
        compiler_params=_cparams("parallel", "arbitrary"),
        name="inproj",
    )(x2d, norm_g, w_main, w_lr)


CONV_PAD = 16


def _conv_kernel(val_ref, glu_ref, gate_ref, cw_ref, cb_ref, lg_ref, lb_ref, out_ref, u_ref):
    s = val_ref.shape[0]
    w = val_ref.shape[1]
    fill_rows = 128
    tile = 32

    u_ref[0:CONV_PAD, :] = jnp.zeros((CONV_PAD, w), F32)
    u_ref[CONV_PAD + s:CONV_PAD + s + CONV_PAD, :] = jnp.zeros((CONV_PAD, w), F32)

    def fill(i, c):
        base = pl.multiple_of(i * fill_rows, fill_rows)
        v = val_ref[pl.ds(base, fill_rows), :].astype(F32)
        g = glu_ref[pl.ds(base, fill_rows), :].astype(F32)
        u_ref[pl.ds(base + CONV_PAD, fill_rows), :] = v * jax.nn.sigmoid(g)
        return c
    lax.fori_loop(0, s // fill_rows, fill, 0)

    win_rows = tile + 2 * CONV_PAD

    def conv(i, c):
        base = pl.multiple_of(i * tile, tile)
        acc = jnp.zeros((tile, w), F32)
        for r in range(SUBLANES):
            offs = [o for o in range(CONV_PAD - CONV_K // 2, CONV_PAD + CONV_K // 2 + 1) if o % SUBLANES == r]
            win = u_ref[pl.ds(base, win_rows), :]
            if r:
                win = pltpu.roll(win, win_rows - r, axis=0)
            for o in offs:
                k = o - (CONV_PAD - CONV_K // 2)
                acc = acc + win[o - r:o - r + tile, :] * cw_ref[k:k + 1, :]
        y = acc + cb_ref[...]
        mu = jnp.mean(y, axis=-1, keepdims=True)
        d = y - mu
        var = jnp.mean(d * d, axis=-1, keepdims=True)
        yn = d * lax.rsqrt(var + EPS) * lg_ref[...] + lb_ref[...]
        gate = gate_ref[pl.ds(base, tile), :].astype(F32)
        out_ref[pl.ds(base, tile), :] = (_silu(yn) * _silu(gate)).astype(BF16)
        return c
    lax.fori_loop(0, s // tile, conv, 0)


def _conv_mixer(z3, conv_w, conv_b, ln_g, ln_b):
    b, s, _ = z3.shape
    w = GROUP_W
    col = lambda j: pl.BlockSpec((None, s, w), lambda i, j=j: (i, 0, j))
    vec = lambda r: pl.BlockSpec((r, w), lambda i: (0, 0))
    return pl.pallas_call(
        _conv_kernel,
        grid=(b,),
        in_specs=[col(0), col(1), col(2), vec(CONV_K), vec(1), vec(1), vec(1)],
        out_specs=pl.BlockSpec((None, s, w), lambda i: (i, 0, 0)),
        out_shape=jax.ShapeDtypeStruct((b, s, w), BF16),
        scratch_shapes=[pltpu.VMEM((s + 2 * CONV_PAD, w), F32)],
        compiler_params=_cparams("parallel"),
        name="conv_mixer",
    )(z3, z3, z3, conv_w, conv_b, ln_g, ln_b)


POOL_PAD = 8


def _pool_kernel(val_ref, gate_ref, pw_ref, ps_ref, out_ref, u_ref):
    s = val_ref.shape[0]
    w = val_ref.shape[1]
    tile = 128

    u_ref[0:POOL_PAD, :] = jnp.zeros((POOL_PAD, w), F32)
    u_ref[POOL_PAD + s:POOL_PAD + s + POOL_PAD, :] = jnp.zeros((POOL_PAD, w), F32)

    def fill(i, c):
        base = pl.multiple_of(i * tile, tile)
        u_ref[pl.ds(base + POOL_PAD, tile), :] = val_ref[pl.ds(base, tile), :].astype(F32)
        return c
    lax.fori_loop(0, s // tile, fill, 0)

    win_rows = tile + 2 * POOL_PAD

    def shifted(a, o):
        return a if o == 0 else pltpu.roll(a, win_rows - o, axis=0)

    def pool(i, c):
        base = pl.multiple_of(i * tile, tile)
        t = base + lax.broadcasted_iota(jnp.int32, (tile, 1), 0)
        ys = []
        for g, win in enumerate(POOL_WINDOWS):
            half = win // 2
            lanes = slice(g * POOL_CG, (g + 1) * POOL_CG)
            x = u_ref[pl.ds(base, win_rows), lanes]
            sm, span = x, 1
            while span < win:
                sm = sm + shifted(sm, span)
                span *= 2
            acc = shifted(sm, POOL_PAD - half)[0:tile]
            cnt = jnp.minimum(t + half, s) - jnp.maximum(t - half, 0)
            centre = x[POOL_PAD:POOL_PAD + tile]
            diff = acc / cnt.astype(F32) - centre
            ys.append(_dot(diff.astype(BF16), pw_ref[g]))
        y = jnp.concatenate(ys, axis=-1) * ps_ref[...]
        gate = gate_ref[pl.ds(base, tile), :].astype(F32)
        out_ref[pl.ds(base, tile), :] = (y * _silu(gate)).astype(BF16)
        return c
    lax.fori_loop(0, s // tile, pool, 0)


def _pool_mixer(z3, pool_w, pool_scale, *, first_block):
    b, s, _ = z3.shape
    w = GROUP_W
    col = lambda j: pl.BlockSpec((None, s, w), lambda i, j=j: (i, 0, j))
    return pl.pallas_call(
        _pool_kernel,
        grid=(b,),
        in_specs=[
            col(first_block), col(first_block + 1),
            pl.BlockSpec(pool_w.shape, lambda i: (0, 0, 0)),
            pl.BlockSpec((1, w), lambda i: (0, 0)),
        ],
        out_specs=pl.BlockSpec((None, s, w), lambda i: (i, 0, 0)),
        out_shape=jax.ShapeDtypeStruct((b, s, w), BF16),
        scratch_shapes=[pltpu.VMEM((s + 2 * POOL_PAD, w), F32)],
        compiler_params=_cparams("parallel"),
        name="pool_mixer",
    )(z3, z3, pool_w, pool_scale)


def _head_rms(x, ones_bd, gain):
    ss = _dot((x * x).astype(BF16), ones_bd)
    return x * lax.rsqrt(ss * (1.0 / NA_HEAD_DIM) + EPS) * gain


def _na_kernel(q_ref, k_ref, v_ref, gate_ref, bias_ref, qg_ref, kg_ref, ones_ref, out_ref,
               kn_ref, *, n_rows, rows_per_step):
    r = pl.program_id(1)
    s = k_ref.shape[0]
    n_keys = NA_ROWS * GRID_W

    @pl.when(r == 0)
    def _():
        rows = 256

        def prep(i, c):
            sl = pl.ds(pl.multiple_of(i * rows, rows), rows)
            k = k_ref[sl, :].astype(F32)
            kn_ref[sl, :] = _head_rms(k, ones_ref[...], kg_ref[...]).astype(BF16)
            return c
        lax.fori_loop(0, s // rows, prep, 0)

    q = q_ref[...].astype(F32)
    qn = (_head_rms(q, ones_ref[...], qg_ref[...]) * (NA_HEAD_DIM ** -0.5)).astype(BF16)
    lane_p = lax.broadcasted_iota(jnp.int32, (1, LANES), 1)
    low_p = lane_p < NA_HEAD_DIM
    gate = gate_ref[...].astype(F32)

    for qi in range(rows_per_step):
        row = r * rows_per_step + qi
        row_start = jnp.clip(row - NA_ROWS // 2, 0, n_rows - NA_ROWS)
        keys = pl.ds(pl.multiple_of(row_start * GRID_W, GRID_W), n_keys)
        qrows = slice(qi * GRID_W, (qi + 1) * GRID_W)
        scores = []
        for p in range(NA_HEADS // 2):
            lanes = slice(p * LANES, (p + 1) * LANES)
            qp = qn[qrows, lanes]
            zero = jnp.zeros_like(qp)
            qm = jnp.concatenate([jnp.where(low_p, qp, zero), jnp.where(low_p, zero, qp)], axis=0)
            scores.append(_dot_nt(qm, kn_ref[keys, lanes]))
        for p in range(NA_HEADS // 2):
            lanes = slice(p * LANES, (p + 1) * LANES)
            sc = scores[p] + bias_ref[row - row_start, p]
            m = jnp.max(sc, axis=-1, keepdims=True)
            e = jnp.exp(sc - m)
            inv = 1.0 / jnp.sum(e, axis=-1, keepdims=True)
            o2 = _dot(e.astype(BF16), v_ref[keys, lanes])
            o = jnp.where(low_p, o2[0:GRID_W] * inv[0:GRID_W], o2[GRID_W:] * inv[GRID_W:])
            out_ref[qrows, lanes] = (o * _silu(gate[qrows, lanes])).astype(BF16)


def _na_bias_table(rpb):
    c = jnp.arange(GRID_W, dtype=jnp.int32)
    cs = jnp.clip(c - NA_COLS // 2, 0, GRID_W - NA_COLS)
    cp = jnp.arange(GRID_W, dtype=jnp.int32)
    valid = (cp[None, :] >= cs[:, None]) & (cp[None, :] < cs[:, None] + NA_COLS)
    col_off = cp[None, :] - c[:, None] + NA_COLS - 1
    offs = jnp.arange(2 * NA_COLS - 1, dtype=jnp.int32)
    pick = ((col_off[None] == offs[:, None, None]) & valid[None]).astype(F32)
    t = jnp.einsum("hro,ocp->hcrp", rpb.astype(F32), pick, precision=lax.Precision.HIGHEST)
    t = jnp.where(valid[None, :, None, :], t, NEG_BIAS)
    tbl = jnp.stack([t[:, :, NA_ROWS - 1 - di:2 * NA_ROWS - 1 - di, :] for di in range(NA_ROWS)])
    return tbl.reshape(NA_ROWS, NA_HEADS // 2, 2 * GRID_W, NA_ROWS * GRID_W)


def _na_mixer(z3, bias_tbl, q_g, k_g, ones_bd, *, first_block):
    b, s, _ = z3.shape
    w = GROUP_W
    n_rows = s // GRID_W
    assert n_rows >= NA_ROWS and n_rows % NA_ROWS_PER_STEP == 0
    fb = first_block
    tq = NA_ROWS_PER_STEP * GRID_W
    const = lambda a: pl.BlockSpec(a.shape, lambda i, r, n=a.ndim: (0,) * n)
    return pl.pallas_call(
        functools.partial(_na_kernel, n_rows=n_rows, rows_per_step=NA_ROWS_PER_STEP),
        grid=(b, n_rows // NA_ROWS_PER_STEP),
        in_specs=[
            pl.BlockSpec((None, tq, w), lambda i, r: (i, r, fb)),
            pl.BlockSpec((None, s, w), lambda i, r: (i, 0, fb + 1)),
            pl.BlockSpec((None, s, w), lambda i, r: (i, 0, fb + 2)),
            pl.BlockSpec((None, tq, w), lambda i, r: (i, r, fb + 3)),
            const(bias_tbl), const(q_g), const(k_g), const(ones_bd),
        ],
        out_specs=pl.BlockSpec((None, tq, w), lambda i, r: (i, r, 0)),
        out_shape=jax.ShapeDtypeStruct((b, s, w), BF16),
        scratch_shapes=[pltpu.VMEM((s, w), BF16)],
        compiler_params=_cparams("parallel", "arbitrary"),
        name="na_mixer",
    )(z3, z3, z3, z3, bias_tbl, q_g, k_g, ones_bd)


def _gla_kernel(q_ref, k_ref, v_ref, gate_ref, lr_ref, a2f_ref, abf_ref, a2b_ref, abb_ref,
                tri_ref, bd_ref, og_ref, out_ref,
                bf_ref, bb_ref, of_ref, ob_ref, stf_ref, stb_ref):
    s = q_ref.shape[0]
    dk_all = q_ref.shape[1]
    dv_all = v_ref.shape[1]
    c = GLA_CHUNK
    n_chunks = s // c
    rows = tri_ref.shape[1]

    def decay(i, carry):
        sl = pl.ds(pl.multiple_of(i * rows, rows), rows)
        lr = lr_ref[sl, :]
        lr_hi = lr.astype(BF16)
        lr_lo = (lr - lr_hi.astype(F32)).astype(BF16)
        for a2_ref, ab_ref, b_ref, tri in ((a2f_ref, abf_ref, bf_ref, tri_ref[0]),
                                          (a2b_ref, abb_ref, bb_ref, tri_ref[1])):
            z = (_dot(lr_hi, a2_ref[0]) + _dot(lr_lo, a2_ref[0]) + _dot(lr_hi, a2_ref[1])) + ab_ref[...]
            g = (jnp.minimum(z, 0.0) - jnp.log1p(jnp.exp(-jnp.abs(z)))) * (1.0 / GLA_TAU)
            g_hi, g_mid, g_lo = _split3(g)
            b_ref[sl, :] = _dot(tri, g_hi) + _dot(tri, g_mid) + _dot(tri, g_lo)
        return carry
    lax.fori_loop(0, s // rows, decay, 0)

    stf_ref[...] = jnp.zeros_like(stf_ref)
    stb_ref[...] = jnp.zeros_like(stb_ref)

    lane = lax.broadcasted_iota(jnp.int32, (1, dk_all), 1)
    head_masks = [(lane // GLA_DK) == h for h in range(GLA_HEADS)]
    ri = lax.broadcasted_iota(jnp.int32, (c, c), 0)
    ci = lax.broadcasted_iota(jnp.int32, (c, c), 1)
    causal = (ri >= ci, ri <= ci)

    def one_chunk(ch, forward):
        b_ref, st_ref, o_ref = (bf_ref, stf_ref, of_ref) if forward else (bb_ref, stb_ref, ob_ref)
        sl = pl.ds(pl.multiple_of(ch * c, c), c)
        b = b_ref[sl, :]
        b_last = b[c - 1:c, :] if forward else b[0:1, :]
        q = q_ref[sl, :].astype(F32)
        k = k_ref[sl, :].astype(F32)
        v = v_ref[sl, :]
        qd = (q * ((GLA_DK ** -0.5) * jnp.exp(b))).astype(BF16)
        ki = (k * jnp.exp(-b)).astype(BF16)
        kd = (k * jnp.exp(b_last - b)).astype(BF16)
        zero = jnp.zeros_like(qd)
        q_stack = jnp.concatenate([jnp.where(hm, qd, zero) for hm in head_masks], axis=0)
        attn = _dot_nt(q_stack, ki)
        mask = causal[0] if forward else causal[1]
        state = st_ref[...]
        o = _dot_nt(qd, state.astype(BF16))
        intra = []
        for h in range(GLA_HEADS):
            a_h = jnp.where(mask, attn[h * c:(h + 1) * c, :], 0.0).astype(BF16)
            intra.append(_dot(a_h, v[:, h * GLA_DV:(h + 1) * GLA_DV]))
        o_ref[sl, :] = o + jnp.concatenate(intra, axis=-1)
        st_ref[...] = state * jnp.exp(b_last) + _dot_tn(v, kd) * bd_ref[...]

    def step(t, carry):
        one_chunk(t, True)
        one_chunk(n_chunks - 1 - t, False)
        return carry
    lax.fori_loop(0, n_chunks, step, 0)

    def finish(i, carry):
        sl = pl.ds(pl.multiple_of(i * rows, rows), rows)
        o = of_ref[sl, :] + ob_ref[sl, :]
        outs = []
        for h in range(GLA_HEADS):
            oh = o[:, h * GLA_DV:(h + 1) * GLA_DV]
            ms = jnp.mean(oh * oh, axis=-1, keepdims=True)
            outs.append(oh * lax.rsqrt(ms + EPS))
        y = jnp.concatenate(outs, axis=-1) * og_ref[...]
        gate = gate_ref[sl, :].astype(F32)
        out_ref[sl, :] = (y * _silu(gate)).astype(BF16)
        return carry
    lax.fori_loop(0, s // rows, finish, 0)


def _gla_mixer(z3, lr3, a2f, abf, a2b, abb, tri, bd_mask, o_g, *, q_block, v_block):
    b, s, _ = z3.shape
    dk_all = GLA_HEADS * GLA_DK
    dv_all = GLA_HEADS * GLA_DV
    full = lambda a: pl.BlockSpec(a.shape, lambda i, n=a.ndim: (0,) * n)
    return pl.pallas_call(
        _gla_kernel,
        grid=(b,),
        in_specs=[
            pl.BlockSpec((None, s, dk_all), lambda i: (i, 0, q_block)),
            pl.BlockSpec((None, s, dk_all), lambda i: (i, 0, q_block + 1)),
            pl.BlockSpec((None, s, dv_all), lambda i: (i, 0, v_block)),
            pl.BlockSpec((None, s, dv_all), lambda i: (i, 0, v_block + 1)),
            pl.BlockSpec((None, s, LANES), lambda i: (i, 0, 0)),
            full(a2f), full(abf), full(a2b), full(abb), full(tri), full(bd_mask), full(o_g),
        ],
        out_specs=pl.BlockSpec((None, s, dv_all), lambda i: (i, 0, 0)),
        out_shape=jax.ShapeDtypeStruct((b, s, dv_all), BF16),
        scratch_shapes=[
            pltpu.VMEM((s, dk_all), F32), pltpu.VMEM((s, dk_all), F32),
            pltpu.VMEM((s, dv_all), F32), pltpu.VMEM((s, dv_all), F32),
            pltpu.VMEM((dv_all, dk_all), F32), pltpu.VMEM((dv_all, dk_all), F32),
        ],
        compiler_params=_cparams("parallel"),
        name="gla_mixer",
    )(z3, z3, z3, z3, lr3, a2f, abf, a2b, abb, tri, bd_mask, o_g)


def _outproj_kernel(x_ref, ya_ref, yb_ref, yc_ref, yd_ref, w_ref, o_ref):
    y = jnp.concatenate([ya_ref[...], yb_ref[...], yc_ref[...], yd_ref[...]], axis=-1)
    o_ref[...] = x_ref[...] + _dot(y, w_ref[...])


def _outproj(x2d, ys, w_out, *, tm):
    m, d = x2d.shape
    gw = ys[0].shape[1]
    yspec = pl.BlockSpec((tm, gw), lambda i: (i, 0))
    return pl.pallas_call(
        _outproj_kernel,
        grid=(m // tm,),
        in_specs=[pl.BlockSpec((tm, d), lambda i: (i, 0)), yspec, yspec, yspec, yspec,
                  pl.BlockSpec(w_out.shape, lambda i: (0, 0))],
        out_specs=pl.BlockSpec((tm, d), lambda i: (i, 0)),
        out_shape=jax.ShapeDtypeStruct((m, d), F32),
        compiler_params=_cparams("parallel"),
        name="outproj",
    )(x2d, *ys, w_out)


def _gla_constants():
    rows = 4 * GLA_CHUNK
    i = jnp.arange(rows)
    same = (i[:, None] // GLA_CHUNK) == (i[None, :] // GLA_CHUNK)
    lower = same & (i[:, None] >= i[None, :])
    upper = same & (i[:, None] <= i[None, :])
    tri = jnp.stack([lower, upper]).astype(BF16)
    rv = jnp.arange(GLA_HEADS * GLA_DV) // GLA_DV
    rk = jnp.arange(GLA_HEADS * GLA_DK) // GLA_DK
    bd_mask = (rv[:, None] == rk[None, :]).astype(F32)
    return tri, bd_mask


def _pad_rank(a2, first_row):
    full = jnp.zeros((LANES, a2.shape[1]), F32).at[first_row:first_row + a2.shape[0]].set(a2.astype(F32))
    hi = full.astype(BF16)
    lo = (full - hi.astype(F32)).astype(BF16)
    return jnp.stack([hi, lo])


def _layer(x, norm_g, w_in, conv_w, conv_b, conv_ln_g, conv_ln_b, na_q_g, na_k_g, na_rpb,
           gla_a2_f, gla_ab_f, gla_a2_b, gla_ab_b, gla_o_g, pool_w, pool_scale, w_out, consts):
    b, s, d = x.shape
    m = b * s
    tri, bd_mask, ones_bd = consts
    lr0 = 7 * GROUP_W + 2 * GLA_HEADS * GLA_DK + 2 * GROUP_W
    lr1 = lr0 + 2 * GLA_RANK
    w_main = jnp.concatenate([w_in[:, :lr0], w_in[:, lr1:]], axis=1).astype(BF16)
    w_lr = jnp.zeros((d, LANES), F32).at[:, :2 * GLA_RANK].set(w_in[:, lr0:lr1]).astype(BF16)

    x2d = x.reshape(m, d)
    z, lr = _inproj(x2d, norm_g.reshape(1, d), w_main, w_lr, **INPROJ_TILES)
    z3 = z.reshape(b, s, z.shape[1])
    lr3 = lr.reshape(b, s, LANES)

    row = lambda a: a.reshape(1, -1).astype(F32)
    y_a = _conv_mixer(z3, conv_w.astype(F32), row(conv_b), row(conv_ln_g), row(conv_ln_b))
    y_b = _na_mixer(z3, _na_bias_table(na_rpb), row(na_q_g), row(na_k_g), ones_bd, first_block=3)
    y_c = _gla_mixer(z3, lr3, _pad_rank(gla_a2_f, 0), row(gla_ab_f), _pad_rank(gla_a2_b, GLA_RANK),
                     row(gla_ab_b), tri, bd_mask, row(gla_o_g), q_block=14, v_block=8)
    y_d = _pool_mixer(z3, pool_w.astype(BF16), row(pool_scale), first_block=10)

    ys = [y.reshape(m, GROUP_W) for y in (y_a, y_b, y_c, y_d)]
    out = _outproj(x2d, ys, w_out.astype(BF16), **OUTPROJ_TILES)
    return out.reshape(b, s, d)


def kernel(x, norm_g, w_in, conv_w, conv_b, conv_ln_g, conv_ln_b, na_q_g, na_k_g, na_rpb, gla_a2_f,
           gla_ab_f, gla_a2_b, gla_ab_b, gla_o_g, pool_w, pool_scale, w_out):
    tri, bd_mask = _gla_constants()
    hq = jnp.arange(GROUP_W) // NA_HEAD_DIM
    ones_bd = (hq[:, None] == hq[None, :]).astype(BF16)
    consts = (tri, bd_mask, ones_bd)
    params = (norm_g, w_in, conv_w, conv_b, conv_ln_g, conv_ln_b, na_q_g, na_k_g, na_rpb, gla_a2_f,
              gla_ab_f, gla_a2_b, gla_ab_b, gla_o_g, pool_w, pool_scale, w_out)
    for l in range(norm_g.shape[0]):
        x = _layer(x, *[p[l] for p in params], consts)
    return x
```

```python
import functools

import jax
import jax.numpy as jnp
from jax import lax
from jax.experimental import pallas as pl
from jax.experimental.pallas import tpu as pltpu

F32 = jnp.float32
BF16 = jnp.bfloat16

EPS = 1e-6
GRID_W = 64
GROUP_W = 512
CONV_K = 31
NA_HEADS = 8
NA_HEAD_DIM = 64
NA_ROWS = 8
NA_COLS = 16
GLA_HEADS = 4
GLA_DK = 64
GLA_DV = 128
GLA_RANK = 16
GLA_TAU = 16.0
GLA_CHUNK = 64
POOL_WINDOWS = (2, 4, 8, 16)
POOL_CG = 128
LANES = 128
SUBLANES = 8
NEG_BIAS = -1e30
NA_ROWS_PER_STEP = 4
NA_SOFTMAX_ROWS = 32
LOG2E = 1.4426950408889634

VMEM_LIMIT = 56 * 1024 * 1024
INPROJ_TILES = dict(tm=1024, tn=1536)
INPROJ_NORM_ROWS = 256
OUTPROJ_TILES = dict(tm=512)


def _cparams(*sem):
    return pltpu.CompilerParams(dimension_semantics=sem, vmem_limit_bytes=VMEM_LIMIT)


def _silu(x):
    return x * jax.nn.sigmoid(x)


def _dot(a, b):
    return jnp.dot(a, b, preferred_element_type=F32)


def _dot_nt(a, b):
    return lax.dot_general(a, b, (((1,), (1,)), ((), ())), preferred_element_type=F32)


def _dot_tn(a, b):
    return lax.dot_general(a, b, (((0,), (0,)), ((), ())), preferred_element_type=F32)


def _split3(x):
    hi = x.astype(BF16)
    r1 = x - hi.astype(F32)
    mid = r1.astype(BF16)
    lo = (r1 - mid.astype(F32)).astype(BF16)
    return hi, mid, lo


def _inproj_kernel(x_ref, g_ref, w_ref, wlr_ref, z_ref, lr_ref, h_ref, *, rows_per_step):
    j = pl.program_id(1)
    tm = x_ref.shape[0]

    @pl.when(j == 0)
    def _():
        for c in range(tm // rows_per_step):
            r = slice(c * rows_per_step, (c + 1) * rows_per_step)
            x = x_ref[r, :]
            ms = jnp.mean(x * x, axis=-1, keepdims=True)
            h = (x * lax.rsqrt(ms + EPS) * g_ref[...]).astype(BF16)
            h_ref[r, :] = h
            lr_ref[r, :] = _dot(h, wlr_ref[...])
            z_ref[r, :] = _dot(h, w_ref[...]).astype(BF16)

    @pl.when(j > 0)
    def _():
        z_ref[...] = _dot(h_ref[...], w_ref[...]).astype(BF16)


def _inproj(x2d, norm_g, w_main, w_lr, *, tm, tn):
    m, d = x2d.shape
    n = w_main.shape[1]
    assert m % tm == 0 and n % tn == 0
    return pl.pallas_call(
        functools.partial(_inproj_kernel, rows_per_step=INPROJ_NORM_ROWS),
        grid=(m // tm, n // tn),
        in_specs=[
            pl.BlockSpec((tm, d), lambda i, j: (i, 0)),
            pl.BlockSpec((1, d), lambda i, j: (0, 0)),
            pl.BlockSpec((d, tn), lambda i, j: (0, j)),
            pl.BlockSpec((d, LANES), lambda i, j: (0, 0)),
        ],
        out_specs=[
            pl.BlockSpec((tm, tn), lambda i, j: (i, j)),
            pl.BlockSpec((tm, LANES), lambda i, j: (i, 0)),
        ],
        out_shape=[
            jax.ShapeDtypeStruct((m, n), BF16),
            jax.ShapeDtypeStruct((m, LANES), F32),
        ],
        scratch_shapes=[pltpu.VMEM((tm, d), BF16)],
        compiler_params=_cparams("parallel", "arbitrary"),
        name="inproj",
    )(x2d, norm_g, w_main, w_lr)


CONV_PAD = 16


def _conv_kernel(val_ref, glu_ref, gate_ref, cw_ref, cb_ref, lg_ref, lb_ref, out_ref, u_ref):
    s = val_ref.shape[0]
    w = val_ref.shape[1]
    fill_rows = 128
    tile = 32

    u_ref[0:CONV_PAD, :] = jnp.zeros((CONV_PAD, w), F32)
    u_ref[CONV_PAD + s:CONV_PAD + s + CONV_PAD, :] = jnp.zeros((CONV_PAD, w), F32)

    def fill(i, c):
        base = pl.multiple_of(i * fill_rows, fill_rows)
        v = val_ref[pl.ds(base, fill_rows), :].astype(F32)
        g = glu_ref[pl.ds(base, fill_rows), :].astype(F32)
        u_ref[pl.ds(base + CONV_PAD, fill_rows), :] = v * jax.nn.sigmoid(g)
        return c
    lax.fori_loop(0, s // fill_rows, fill, 0)

    win_rows = tile + 2 * CONV_PAD

    def conv(i, c):
        base = pl.multiple_of(i * tile, tile)
        acc = jnp.zeros((tile, w), F32)
        for r in range(SUBLANES):
            offs = [o for o in range(CONV_PAD - CONV_K // 2, CONV_PAD + CONV_K // 2 + 1) if o % SUBLANES == r]
            win = u_ref[pl.ds(base, win_rows), :]
            if r:
                win = pltpu.roll(win, win_rows - r, axis=0)
            for o in offs:
                k = o - (CONV_PAD - CONV_K // 2)
                acc = acc + win[o - r:o - r + tile, :] * cw_ref[k:k + 1, :]
        y = acc + cb_ref[...]
        mu = jnp.mean(y, axis=-1, keepdims=True)
        d = y - mu
        var = jnp.mean(d * d, axis=-1, keepdims=True)
        yn = d * lax.rsqrt(var + EPS) * lg_ref[...] + lb_ref[...]
        gate = gate_ref[pl.ds(base, tile), :].astype(F32)
        out_ref[pl.ds(base, tile), :] = (_silu(yn) * _silu(gate)).astype(BF16)
        return c
    lax.fori_loop(0, s // tile, conv, 0, unroll=2)


def _conv_mixer(z3, conv_w, conv_b, ln_g, ln_b):
    b, s, _ = z3.shape
    w = GROUP_W
    col = lambda j: pl.BlockSpec((None, s, w), lambda i, j=j: (i, 0, j))
    vec = lambda r: pl.BlockSpec((r, w), lambda i: (0, 0))
    return pl.pallas_call(
        _conv_kernel,
        grid=(b,),
        in_specs=[col(0), col(1), col(2), vec(CONV_K), vec(1), vec(1), vec(1)],
        out_specs=pl.BlockSpec((None, s, w), lambda i: (i, 0, 0)),
        out_shape=jax.ShapeDtypeStruct((b, s, w), BF16),
        scratch_shapes=[pltpu.VMEM((s + 2 * CONV_PAD, w), F32)],
        compiler_params=_cparams("parallel"),
        name="conv_mixer",
    )(z3, z3, z3, conv_w, conv_b, ln_g, ln_b)


POOL_PAD = 8


def _pool_kernel(val_ref, gate_ref, pw_ref, ps_ref, out_ref, u_ref):
    s = val_ref.shape[0]
    w = val_ref.shape[1]
    tile = 128

    u_ref[0:POOL_PAD, :] = jnp.zeros((POOL_PAD, w), F32)
    u_ref[POOL_PAD + s:POOL_PAD + s + POOL_PAD, :] = jnp.zeros((POOL_PAD, w), F32)

    def fill(i, c):
        base = pl.multiple_of(i * tile, tile)
        u_ref[pl.ds(base + POOL_PAD, tile), :] = val_ref[pl.ds(base, tile), :].astype(F32)
        return c
    lax.fori_loop(0, s // tile, fill, 0)

    win_rows = tile + 2 * POOL_PAD

    def shifted(a, o):
        return a if o == 0 else pltpu.roll(a, win_rows - o, axis=0)

    def pool(i, c):
        base = pl.multiple_of(i * tile, tile)
        t = base + lax.broadcasted_iota(jnp.int32, (tile, 1), 0)
        ys = []
        for g, win in enumerate(POOL_WINDOWS):
            half = win // 2
            lanes = slice(g * POOL_CG, (g + 1) * POOL_CG)
            x = u_ref[pl.ds(base, win_rows), lanes]
            sm, span = x, 1
            while span < win:
                sm = sm + shifted(sm, span)
                span *= 2
            acc = shifted(sm, POOL_PAD - half)[0:tile]
            cnt = jnp.minimum(t + half, s) - jnp.maximum(t - half, 0)
            centre = x[POOL_PAD:POOL_PAD + tile]
            diff = acc / cnt.astype(F32) - centre
            ys.append(_dot(diff.astype(BF16), pw_ref[g]))
        y = jnp.concatenate(ys, axis=-1) * ps_ref[...]
        gate = gate_ref[pl.ds(base, tile), :].astype(F32)
        out_ref[pl.ds(base, tile), :] = (y * _silu(gate)).astype(BF16)
        return c
    lax.fori_loop(0, s // tile, pool, 0)


def _pool_mixer(z3, pool_w, pool_scale, *, first_block):
    b, s, _ = z3.shape
    w = GROUP_W
    col = lambda j: pl.BlockSpec((None, s, w), lambda i, j=j: (i, 0, j))
    return pl.pallas_call(
        _pool_kernel,
        grid=(b,),
        in_specs=[
            col(first_block), col(first_block + 1),
            pl.BlockSpec(pool_w.shape, lambda i: (0, 0, 0)),
            pl.BlockSpec((1, w), lambda i: (0, 0)),
        ],
        out_specs=pl.BlockSpec((None, s, w), lambda i: (i, 0, 0)),
        out_shape=jax.ShapeDtypeStruct((b, s, w), BF16),
        scratch_shapes=[pltpu.VMEM((s + 2 * POOL_PAD, w), F32)],
        compiler_params=_cparams("parallel"),
        name="pool_mixer",
    )(z3, z3, pool_w, pool_scale)


def _head_rms(x, ones_bd, gain):
    ss = _dot((x * x).astype(BF16), ones_bd)
    return x * lax.rsqrt(ss * (1.0 / NA_HEAD_DIM) + EPS) * gain


def _na_kernel(q_ref, k_ref, v_ref, gate_ref, bias_ref, qg_ref, kg_ref, ones_ref, out_ref,
               kn_ref, *, n_rows, rows_per_step):
    r = pl.program_id(1)
    s = k_ref.shape[0]
    n_keys = NA_ROWS * GRID_W

    @pl.when(r == 0)
    def _():
        rows = 256

        def prep(i, c):
            sl = pl.ds(pl.multiple_of(i * rows, rows), rows)
            k = k_ref[sl, :].astype(F32)
            kn_ref[sl, :] = _head_rms(k, ones_ref[...], kg_ref[...]).astype(BF16)
            return c
        lax.fori_loop(0, s // rows, prep, 0)

    q = q_ref[...].astype(F32)
    qn = (_head_rms(q, ones_ref[...], qg_ref[...]) * (LOG2E * NA_HEAD_DIM ** -0.5)).astype(BF16)
    lane_p = lax.broadcasted_iota(jnp.int32, (1, LANES), 1)
    low_p = lane_p < NA_HEAD_DIM
    gate = gate_ref[...].astype(F32)

    for qi in range(rows_per_step):
        row = r * rows_per_step + qi
        row_start = jnp.clip(row - NA_ROWS // 2, 0, n_rows - NA_ROWS)
        keys = pl.ds(pl.multiple_of(row_start * GRID_W, GRID_W), n_keys)
        qrows = slice(qi * GRID_W, (qi + 1) * GRID_W)
        scores = []
        for p in range(NA_HEADS // 2):
            lanes = slice(p * LANES, (p + 1) * LANES)
            qp = qn[qrows, lanes]
            zero = jnp.zeros_like(qp)
            qm = jnp.concatenate([jnp.where(low_p, qp, zero), jnp.where(low_p, zero, qp)], axis=0)
            scores.append(_dot_nt(qm, kn_ref[keys, lanes]))
        for p in range(NA_HEADS // 2):
            lanes = slice(p * LANES, (p + 1) * LANES)
            es, invs = [], []
            for g in range(2 * GRID_W // NA_SOFTMAX_ROWS):
                rows = slice(g * NA_SOFTMAX_ROWS, (g + 1) * NA_SOFTMAX_ROWS)
                sc = scores[p][rows] + bias_ref[row - row_start, p, rows, :]
                m = jnp.max(sc, axis=-1, keepdims=True)
                e = jnp.exp2(sc - m)
                invs.append(1.0 / jnp.sum(e, axis=-1, keepdims=True))
                es.append(e.astype(BF16))
            inv = jnp.concatenate(invs, axis=0)
            o2 = _dot(jnp.concatenate(es, axis=0), v_ref[keys, lanes])
            o = jnp.where(low_p, o2[0:GRID_W] * inv[0:GRID_W], o2[GRID_W:] * inv[GRID_W:])
            out_ref[qrows, lanes] = (o * _silu(gate[qrows, lanes])).astype(BF16)


def _na_bias_table(rpb):
    c = jnp.arange(GRID_W, dtype=jnp.int32)
    cs = jnp.clip(c - NA_COLS // 2, 0, GRID_W - NA_COLS)
    cp = jnp.arange(GRID_W, dtype=jnp.int32)
    valid = (cp[None, :] >= cs[:, None]) & (cp[None, :] < cs[:, None] + NA_COLS)
    col_off = cp[None, :] - c[:, None] + NA_COLS - 1
    offs = jnp.arange(2 * NA_COLS - 1, dtype=jnp.int32)
    pick = ((col_off[None] == offs[:, None, None]) & valid[None]).astype(F32)
    t = jnp.einsum("hro,ocp->hcrp", rpb.astype(F32), pick, precision=lax.Precision.HIGHEST)
    t = jnp.where(valid[None, :, None, :], t * LOG2E, NEG_BIAS)
    tbl = jnp.stack([t[:, :, NA_ROWS - 1 - di:2 * NA_ROWS - 1 - di, :] for di in range(NA_ROWS)])
    return tbl.reshape(NA_ROWS, NA_HEADS // 2, 2 * GRID_W, NA_ROWS * GRID_W)


def _na_mixer(z3, bias_tbl, q_g, k_g, ones_bd, *, first_block):
    b, s, _ = z3.shape
    w = GROUP_W
    n_rows = s // GRID_W
    assert n_rows >= NA_ROWS and n_rows % NA_ROWS_PER_STEP == 0
    fb = first_block
    tq = NA_ROWS_PER_STEP * GRID_W
    const = lambda a: pl.BlockSpec(a.shape, lambda i, r, n=a.ndim: (0,) * n)
    return pl.pallas_call(
        functools.partial(_na_kernel, n_rows=n_rows, rows_per_step=NA_ROWS_PER_STEP),
        grid=(b, n_rows // NA_ROWS_PER_STEP),
        in_specs=[
            pl.BlockSpec((None, tq, w), lambda i, r: (i, r, fb)),
            pl.BlockSpec((None, s, w), lambda i, r: (i, 0, fb + 1)),
            pl.BlockSpec((None, s, w), lambda i, r: (i, 0, fb + 2)),
            pl.BlockSpec((None, tq, w), lambda i, r: (i, r, fb + 3)),
            const(bias_tbl), const(q_g), const(k_g), const(ones_bd),
        ],
        out_specs=pl.BlockSpec((None, tq, w), lambda i, r: (i, r, 0)),
        out_shape=jax.ShapeDtypeStruct((b, s, w), BF16),
        scratch_shapes=[pltpu.VMEM((s, w), BF16)],
        compiler_params=_cparams("parallel", "arbitrary"),
        name="na_mixer",
    )(z3, z3, z3, z3, bias_tbl, q_g, k_g, ones_bd)


def _gla_kernel(q_ref, k_ref, v_ref, gate_ref, lr_ref, a2f_ref, abf_ref, a2b_ref, abb_ref,
                tri_ref, bd_ref, og_ref, out_ref,
                bf_ref, bb_ref, of_ref, ob_ref, stf_ref, stb_ref):
    s = q_ref.shape[0]
    dk_all = q_ref.shape[1]
    dv_all = v_ref.shape[1]
    c = GLA_CHUNK
    n_chunks = s // c
    rows = tri_ref.shape[1]

    def decay(i, carry):
        sl = pl.ds(pl.multiple_of(i * rows, rows), rows)
        lr = lr_ref[sl, :]
        lr_hi = lr.astype(BF16)
        lr_lo = (lr - lr_hi.astype(F32)).astype(BF16)
        for a2_ref, ab_ref, b_ref, tri in ((a2f_ref, abf_ref, bf_ref, tri_ref[0]),
                                          (a2b_ref, abb_ref, bb_ref, tri_ref[1])):
            z = (_dot(lr_hi, a2_ref[0]) + _dot(lr_lo, a2_ref[0]) + _dot(lr_hi, a2_ref[1])) + ab_ref[...]
            g = (jnp.minimum(z, 0.0) - jnp.log1p(jnp.exp(-jnp.abs(z)))) * (1.0 / GLA_TAU)
            g_hi, g_mid, g_lo = _split3(g)
            b_ref[sl, :] = _dot(tri, g_hi) + _dot(tri, g_mid) + _dot(tri, g_lo)
        return carry
    lax.fori_loop(0, s // rows, decay, 0)

    stf_ref[...] = jnp.zeros_like(stf_ref)
    stb_ref[...] = jnp.zeros_like(stb_ref)

    lane = lax.broadcasted_iota(jnp.int32, (1, dk_all), 1)
    head_masks = [(lane // GLA_DK) == h for h in range(GLA_HEADS)]
    ri = lax.broadcasted_iota(jnp.int32, (c, c), 0)
    ci = lax.broadcasted_iota(jnp.int32, (c, c), 1)
    causal = (ri >= ci, ri <= ci)

    def one_chunk(ch, forward):
        b_ref, st_ref, o_ref = (bf_ref, stf_ref, of_ref) if forward else (bb_ref, stb_ref, ob_ref)
        sl = pl.ds(pl.multiple_of(ch * c, c), c)
        b = b_ref[sl, :]
        b_last = b[c - 1:c, :] if forward else b[0:1, :]
        q = q_ref[sl, :].astype(F32)
        k = k_ref[sl, :].astype(F32)
        v = v_ref[sl, :]
        qd = (q * ((GLA_DK ** -0.5) * jnp.exp(b))).astype(BF16)
        ki = (k * jnp.exp(-b)).astype(BF16)
        kd = (k * jnp.exp(b_last - b)).astype(BF16)
        zero = jnp.zeros_like(qd)
        q_stack = jnp.concatenate([jnp.where(hm, qd, zero) for hm in head_masks], axis=0)
        attn = _dot_nt(q_stack, ki)
        mask = causal[0] if forward else causal[1]
        state = st_ref[...]
        o = _dot_nt(qd, state.astype(BF16))
        intra = []
        for h in range(GLA_HEADS):
            a_h = jnp.where(mask, attn[h * c:(h + 1) * c, :], 0.0).astype(BF16)
            intra.append(_dot(a_h, v[:, h * GLA_DV:(h + 1) * GLA_DV]))
        o_ref[sl, :] = o + jnp.concatenate(intra, axis=-1)
        st_ref[...] = state * jnp.exp(b_last) + _dot_tn(v, kd) * bd_ref[...]

    def step(t, carry):
        one_chunk(t, True)
        one_chunk(n_chunks - 1 - t, False)
        return carry
    lax.fori_loop(0, n_chunks, step, 0, unroll=2)

    def finish(i, carry):
        sl = pl.ds(pl.multiple_of(i * rows, rows), rows)
        o = of_ref[sl, :] + ob_ref[sl, :]
        outs = []
        for h in range(GLA_HEADS):
            oh = o[:, h * GLA_DV:(h + 1) * GLA_DV]
            ms = jnp.mean(oh * oh, axis=-1, keepdims=True)
            outs.append(oh * lax.rsqrt(ms + EPS))
        y = jnp.concatenate(outs, axis=-1) * og_ref[...]
        gate = gate_ref[sl, :].astype(F32)
        out_ref[sl, :] = (y * _silu(gate)).astype(BF16)
        return carry
    lax.fori_loop(0, s // rows, finish, 0)


def _gla_mixer(z3, lr3, a2f, abf, a2b, abb, tri, bd_mask, o_g, *, q_block, v_block):
    b, s, _ = z3.shape
    dk_all = GLA_HEADS * GLA_DK
    dv_all = GLA_HEADS * GLA_DV
    full = lambda a: pl.BlockSpec(a.shape, lambda i, n=a.ndim: (0,) * n)
    return pl.pallas_call(
        _gla_kernel,
        grid=(b,),
        in_specs=[
            pl.BlockSpec((None, s, dk_all), lambda i: (i, 0, q_block)),
            pl.BlockSpec((None, s, dk_all), lambda i: (i, 0, q_block + 1)),
            pl.BlockSpec((None, s, dv_all), lambda i: (i, 0, v_block)),
            pl.BlockSpec((None, s, dv_all), lambda i: (i, 0, v_block + 1)),
            pl.BlockSpec((None, s, LANES), lambda i: (i, 0, 0)),
            full(a2f), full(abf), full(a2b), full(abb), full(tri), full(bd_mask), full(o_g),
        ],
        out_specs=pl.BlockSpec((None, s, dv_all), lambda i: (i, 0, 0)),
        out_shape=jax.ShapeDtypeStruct((b, s, dv_all), BF16),
        scratch_shapes=[
            pltpu.VMEM((s, dk_all), F32), pltpu.VMEM((s, dk_all), F32),
            pltpu.VMEM((s, dv_all), F32), pltpu.VMEM((s, dv_all), F32),
            pltpu.VMEM((dv_all, dk_all), F32), pltpu.VMEM((dv_all, dk_all), F32),
        ],
        compiler_params=_cparams("parallel"),
        name="gla_mixer",
    )(z3, z3, z3, z3, lr3, a2f, abf, a2b, abb, tri, bd_mask, o_g)


def _outproj_kernel(x_ref, ya_ref, yb_ref, yc_ref, yd_ref, w_ref, o_ref):
    y = jnp.concatenate([ya_ref[...], yb_ref[...], yc_ref[...], yd_ref[...]], axis=-1)
    o_ref[...] = x_ref[...] + _dot(y, w_ref[...])


def _outproj(x2d, ys, w_out, *, tm):
    m, d = x2d.shape
    gw = ys[0].shape[1]
    yspec = pl.BlockSpec((tm, gw), lambda i: (i, 0))
    return pl.pallas_call(
        _outproj_kernel,
        grid=(m // tm,),
        in_specs=[pl.BlockSpec((tm, d), lambda i: (i, 0)), yspec, yspec, yspec, yspec,
                  pl.BlockSpec(w_out.shape, lambda i: (0, 0))],
        out_specs=pl.BlockSpec((tm, d), lambda i: (i, 0)),
        out_shape=jax.ShapeDtypeStruct((m, d), F32),
        compiler_params=_cparams("parallel"),
        name="outproj",
    )(x2d, *ys, w_out)


def _gla_constants():
    rows = 4 * GLA_CHUNK
    i = jnp.arange(rows)
    same = (i[:, None] // GLA_CHUNK) == (i[None, :] // GLA_CHUNK)
    lower = same & (i[:, None] >= i[None, :])
    upper = same & (i[:, None] <= i[None, :])
    tri = jnp.stack([lower, upper]).astype(BF16)
    rv = jnp.arange(GLA_HEADS * GLA_DV) // GLA_DV
    rk = jnp.arange(GLA_HEADS * GLA_DK) // GLA_DK
    bd_mask = (rv[:, None] == rk[None, :]).astype(F32)
    return tri, bd_mask


def _pad_rank(a2, first_row):
    full = jnp.zeros((LANES, a2.shape[1]), F32).at[first_row:first_row + a2.shape[0]].set(a2.astype(F32))
    hi = full.astype(BF16)
    lo = (full - hi.astype(F32)).astype(BF16)
    return jnp.stack([hi, lo])


def _layer(x, norm_g, w_in, conv_w, conv_b, conv_ln_g, conv_ln_b, na_q_g, na_k_g, na_rpb,
           gla_a2_f, gla_ab_f, gla_a2_b, gla_ab_b, gla_o_g, pool_w, pool_scale, w_out, consts):
    b, s, d = x.shape
    m = b * s
    tri, bd_mask, ones_bd = consts
    lr0 = 7 * GROUP_W + 2 * GLA_HEADS * GLA_DK + 2 * GROUP_W
    lr1 = lr0 + 2 * GLA_RANK
    w_main = jnp.concatenate([w_in[:, :lr0], w_in[:, lr1:]], axis=1).astype(BF16)
    w_lr = jnp.zeros((d, LANES), F32).at[:, :2 * GLA_RANK].set(w_in[:, lr0:lr1]).astype(BF16)

    x2d = x.reshape(m, d)
    z, lr = _inproj(x2d, norm_g.reshape(1, d), w_main, w_lr, **INPROJ_TILES)
    z3 = z.reshape(b, s, z.shape[1])
    lr3 = lr.reshape(b, s, LANES)

    row = lambda a: a.reshape(1, -1).astype(F32)
    y_a = _conv_mixer(z3, conv_w.astype(F32), row(conv_b), row(conv_ln_g), row(conv_ln_b))
    y_b = _na_mixer(z3, _na_bias_table(na_rpb), row(na_q_g), row(na_k_g), ones_bd, first_block=3)
    y_c = _gla_mixer(z3, lr3, _pad_rank(gla_a2_f, 0), row(gla_ab_f), _pad_rank(gla_a2_b, GLA_RANK),
                     row(gla_ab_b), tri, bd_mask, row(gla_o_g), q_block=14, v_block=8)
    y_d = _pool_mixer(z3, pool_w.astype(BF16), row(pool_scale), first_block=10)

    ys = [y.reshape(m, GROUP_W) for y in (y_a, y_b, y_c, y_d)]
    out = _outproj(x2d, ys, w_out.astype(BF16), **OUTPROJ_TILES)
    return out.reshape(b, s, d)


def kernel(x, norm_g, w_in, conv_w, conv_b, conv_ln_g, conv_ln_b, na_q_g, na_k_g, na_rpb, gla_a2_f,
           gla_ab_f, gla_a2_b, gla_ab_b, gla_o_g, pool_w, pool_scale, w_out):
    tri, bd_mask = _gla_constants()
    hq = jnp.arange(GROUP_W) // NA_HEAD_DIM
    ones_bd = (hq[:, None] == hq[None, :]).astype(BF16)
    consts = (tri, bd_mask, ones_bd)
    params = (norm_g, w_in, conv_w, conv_b, conv_ln_g, conv_ln_b, na_q_g, na_k_g, na_rpb, gla_a2_f,
              gla_ab_f, gla_a2_b, gla_ab_b, gla_o_g, pool_w, pool_scale, w_out)
    for l in range(norm_g.shape[0]):
        x = _layer(x, *[p[l] for p in params], consts)
    return x
```

```python
import functools

import jax
import jax.numpy as jnp
from jax import lax
from jax.experimental import pallas as pl
from jax.experimental.pallas import tpu as pltpu

F32 = jnp.float32
BF16 = jnp.bfloat16

EPS = 1e-6
GRID_W = 64
GROUP_W = 512
CONV_K = 31
NA_HEADS = 8
NA_HEAD_DIM = 64
NA_ROWS = 8
NA_COLS = 16
GLA_HEADS = 4
GLA_DK = 64
GLA_DV = 128
GLA_RANK = 16
GLA_TAU = 16.0
GLA_CHUNK = 64
POOL_WINDOWS = (2, 4, 8, 16)
POOL_CG = 128
LANES = 128
SUBLANES = 8
NEG_BIAS = -1e30
NA_ROWS_PER_STEP = 4
NA_SOFTMAX_ROWS = 32
LOG2E = 1.4426950408889634

VMEM_LIMIT = 56 * 1024 * 1024
INPROJ_TILES = dict(tm=1024, tn=2048)
INPROJ_NORM_ROWS = 256
OUTPROJ_TILES = dict(tm=512)


def _cparams(*sem):
    return pltpu.CompilerParams(dimension_semantics=sem, vmem_limit_bytes=VMEM_LIMIT)


def _silu(x):
    return x * jax.nn.sigmoid(x)


def _dot(a, b):
    return jnp.dot(a, b, preferred_element_type=F32)


def _dot_nt(a, b):
    return lax.dot_general(a, b, (((1,), (1,)), ((), ())), preferred_element_type=F32)


def _dot_tn(a, b):
    return lax.dot_general(a, b, (((0,), (0,)), ((), ())), preferred_element_type=F32)


def _split3(x):
    hi = x.astype(BF16)
    r1 = x - hi.astype(F32)
    mid = r1.astype(BF16)
    lo = (r1 - mid.astype(F32)).astype(BF16)
    return hi, mid, lo


def _inproj_kernel(x_ref, g_ref, w_ref, wlr_ref, z_ref, lr_ref, h_ref, *, rows_per_step):
    j = pl.program_id(1)
    tm = x_ref.shape[0]

    @pl.when(j == 0)
    def _():
        for c in range(tm // rows_per_step):
            r = slice(c * rows_per_step, (c + 1) * rows_per_step)
            x = x_ref[r, :]
            ms = jnp.mean(x * x, axis=-1, keepdims=True)
            h = (x * lax.rsqrt(ms + EPS) * g_ref[...]).astype(BF16)
            h_ref[r, :] = h
            lr_ref[r, :] = _dot(h, wlr_ref[...])
            z_ref[r, :] = _dot(h, w_ref[...]).astype(BF16)

    @pl.when(j > 0)
    def _():
        z_ref[...] = _dot(h_ref[...], w_ref[...]).astype(BF16)


def _inproj(x2d, norm_g, w_main, w_lr, *, tm, tn):
    m, d = x2d.shape
    n = w_main.shape[1]
    assert m % tm == 0 and n % tn == 0
    return pl.pallas_call(
        functools.partial(_inproj_kernel, rows_per_step=INPROJ_NORM_ROWS),
        grid=(m // tm, n // tn),
        in_specs=[
            pl.BlockSpec((tm, d), lambda i, j: (i, 0)),
            pl.BlockSpec((1, d), lambda i, j: (0, 0)),
            pl.BlockSpec((d, tn), lambda i, j: (0, j)),
            pl.BlockSpec((d, LANES), lambda i, j: (0, 0)),
        ],
        out_specs=[
            pl.BlockSpec((tm, tn), lambda i, j: (i, j)),
            pl.BlockSpec((tm, LANES), lambda i, j: (i, 0)),
        ],
        out_shape=[
            jax.ShapeDtypeStruct((m, n), BF16),
            jax.ShapeDtypeStruct((m, LANES), F32),
        ],
        scratch_shapes=[pltpu.VMEM((tm, d), BF16)],
        compiler_params=_cparams("parallel", "arbitrary"),
        name="inproj",
    )(x2d, norm_g, w_main, w_lr)


CONV_PAD = 16


def _conv_kernel(val_ref, glu_ref, gate_ref, cw_ref, cb_ref, lg_ref, lb_ref, out_ref, u_ref):
    s = val_ref.shape[0]
    w = val_ref.shape[1]
    fill_rows = 128
    tile = 32

    u_ref[0:CONV_PAD, :] = jnp.zeros((CONV_PAD, w), F32)
    u_ref[CONV_PAD + s:CONV_PAD + s + CONV_PAD, :] = jnp.zeros((CONV_PAD, w), F32)

    def fill(i, c):
        base = pl.multiple_of(i * fill_rows, fill_rows)
        v = val_ref[pl.ds(base, fill_rows), :].astype(F32)
        g = glu_ref[pl.ds(base, fill_rows), :].astype(F32)
        u_ref[pl.ds(base + CONV_PAD, fill_rows), :] = v * jax.nn.sigmoid(g)
        return c
    lax.fori_loop(0, s // fill_rows, fill, 0)

    win_rows = tile + 2 * CONV_PAD

    def conv(i, c):
        base = pl.multiple_of(i * tile, tile)
        acc = jnp.zeros((tile, w), F32)
        for r in range(SUBLANES):
            offs = [o for o in range(CONV_PAD - CONV_K // 2, CONV_PAD + CONV_K // 2 + 1) if o % SUBLANES == r]
            win = u_ref[pl.ds(base, win_rows), :]
            if r:
                win = pltpu.roll(win, win_rows - r, axis=0)
            for o in offs:
                k = o - (CONV_PAD - CONV_K // 2)
                wk = pltpu.repeat(cw_ref[k * SUBLANES:(k + 1) * SUBLANES, :], tile // SUBLANES, axis=0)
                acc = acc + win[o - r:o - r + tile, :] * wk
        y = acc + cb_ref[...]
        mu = jnp.mean(y, axis=-1, keepdims=True)
        d = y - mu
        var = jnp.mean(d * d, axis=-1, keepdims=True)
        yn = d * lax.rsqrt(var + EPS) * lg_ref[...] + lb_ref[...]
        gate = gate_ref[pl.ds(base, tile), :].astype(F32)
        out_ref[pl.ds(base, tile), :] = (_silu(yn) * _silu(gate)).astype(BF16)
        return c
    lax.fori_loop(0, s // tile, conv, 0, unroll=2)


def _conv_mixer(z3, conv_w, conv_b, ln_g, ln_b):
    b, s, _ = z3.shape
    w = GROUP_W
    col = lambda j: pl.BlockSpec((None, s, w), lambda i, j=j: (i, 0, j))
    vec = lambda r: pl.BlockSpec((r, w), lambda i: (0, 0))
    return pl.pallas_call(
        _conv_kernel,
        grid=(b,),
        in_specs=[col(0), col(1), col(2), vec(CONV_K * SUBLANES), vec(1), vec(1), vec(1)],
        out_specs=pl.BlockSpec((None, s, w), lambda i: (i, 0, 0)),
        out_shape=jax.ShapeDtypeStruct((b, s, w), BF16),
        scratch_shapes=[pltpu.VMEM((s + 2 * CONV_PAD, w), F32)],
        compiler_params=_cparams("parallel"),
        name="conv_mixer",
    )(z3, z3, z3, conv_w, conv_b, ln_g, ln_b)


POOL_PAD = 8


def _pool_kernel(val_ref, gate_ref, pw_ref, ps_ref, out_ref, u_ref):
    s = val_ref.shape[0]
    w = val_ref.shape[1]
    tile = 128

    u_ref[0:POOL_PAD, :] = jnp.zeros((POOL_PAD, w), F32)
    u_ref[POOL_PAD + s:POOL_PAD + s + POOL_PAD, :] = jnp.zeros((POOL_PAD, w), F32)

    def fill(i, c):
        base = pl.multiple_of(i * tile, tile)
        u_ref[pl.ds(base + POOL_PAD, tile), :] = val_ref[pl.ds(base, tile), :].astype(F32)
        return c
    lax.fori_loop(0, s // tile, fill, 0)

    win_rows = tile + 2 * POOL_PAD

    def shifted(a, o):
        return a if o == 0 else pltpu.roll(a, win_rows - o, axis=0)

    def pool(i, c):
        base = pl.multiple_of(i * tile, tile)
        t = base + lax.broadcasted_iota(jnp.int32, (tile, 1), 0)
        ys = []
        for g, win in enumerate(POOL_WINDOWS):
            half = win // 2
            lanes = slice(g * POOL_CG, (g + 1) * POOL_CG)
            x = u_ref[pl.ds(base, win_rows), lanes]
            sm, span = x, 1
            while span < win:
                sm = sm + shifted(sm, span)
                span *= 2
            acc = shifted(sm, POOL_PAD - half)[0:tile]
            cnt = jnp.minimum(t + half, s) - jnp.maximum(t - half, 0)
            centre = x[POOL_PAD:POOL_PAD + tile]
            diff = acc / cnt.astype(F32) - centre
            ys.append(_dot(diff.astype(BF16), pw_ref[g]))
        y = jnp.concatenate(ys, axis=-1) * ps_ref[...]
        gate = gate_ref[pl.ds(base, tile), :].astype(F32)
        out_ref[pl.ds(base, tile), :] = (y * _silu(gate)).astype(BF16)
        return c
    lax.fori_loop(0, s // tile, pool, 0, unroll=2)


def _pool_mixer(z3, pool_w, pool_scale, *, first_block):
    b, s, _ = z3.shape
    w = GROUP_W
    col = lambda j: pl.BlockSpec((None, s, w), lambda i, j=j: (i, 0, j))
    return pl.pallas_call(
        _pool_kernel,
        grid=(b,),
        in_specs=[
            col(first_block), col(first_block + 1),
            pl.BlockSpec(pool_w.shape, lambda i: (0, 0, 0)),
            pl.BlockSpec((1, w), lambda i: (0, 0)),
        ],
        out_specs=pl.BlockSpec((None, s, w), lambda i: (i, 0, 0)),
        out_shape=jax.ShapeDtypeStruct((b, s, w), BF16),
        scratch_shapes=[pltpu.VMEM((s + 2 * POOL_PAD, w), F32)],
        compiler_params=_cparams("parallel"),
        name="pool_mixer",
    )(z3, z3, pool_w, pool_scale)


def _head_rms(x, ones_bd, gain):
    ss = _dot((x * x).astype(BF16), ones_bd)
    return x * lax.rsqrt(ss * (1.0 / NA_HEAD_DIM) + EPS) * gain


def _na_kernel(q_ref, k_ref, v_ref, gate_ref, bias_ref, qg_ref, kg_ref, ones_ref, out_ref,
               kn_ref, *, n_rows, rows_per_step):
    r = pl.program_id(1)
    s = k_ref.shape[0]
    n_keys = NA_ROWS * GRID_W

    @pl.when(r == 0)
    def _():
        rows = 256

        def prep(i, c):
            sl = pl.ds(pl.multiple_of(i * rows, rows), rows)
            k = k_ref[sl, :].astype(F32)
            kn_ref[sl, :] = _head_rms(k, ones_ref[...], kg_ref[...]).astype(BF16)
            return c
        lax.fori_loop(0, s // rows, prep, 0)

    q = q_ref[...].astype(F32)
    qn = (_head_rms(q, ones_ref[...], qg_ref[...]) * (LOG2E * NA_HEAD_DIM ** -0.5)).astype(BF16)
    lane_p = lax.broadcasted_iota(jnp.int32, (1, LANES), 1)
    low_p = lane_p < NA_HEAD_DIM
    gate = gate_ref[...].astype(F32)

    for qi in range(rows_per_step):
        row = r * rows_per_step + qi
        row_start = jnp.clip(row - NA_ROWS // 2, 0, n_rows - NA_ROWS)
        keys = pl.ds(pl.multiple_of(row_start * GRID_W, GRID_W), n_keys)
        qrows = slice(qi * GRID_W, (qi + 1) * GRID_W)
        scores = []
        for p in range(NA_HEADS // 2):
            lanes = slice(p * LANES, (p + 1) * LANES)
            qp = qn[qrows, lanes]
            zero = jnp.zeros_like(qp)
            qm = jnp.concatenate([jnp.where(low_p, qp, zero), jnp.where(low_p, zero, qp)], axis=0)
            scores.append(_dot_nt(qm, kn_ref[keys, lanes]))
        for p in range(NA_HEADS // 2):
            lanes = slice(p * LANES, (p + 1) * LANES)
            es, invs = [], []
            for g in range(2 * GRID_W // NA_SOFTMAX_ROWS):
                rows = slice(g * NA_SOFTMAX_ROWS, (g + 1) * NA_SOFTMAX_ROWS)
                sc = scores[p][rows] + bias_ref[row - row_start, p, rows, :]
                m = jnp.max(sc, axis=-1, keepdims=True)
                e = jnp.exp2(sc - m)
                invs.append(1.0 / jnp.sum(e, axis=-1, keepdims=True))
                es.append(e.astype(BF16))
            inv = jnp.concatenate(invs, axis=0)
            o2 = _dot(jnp.concatenate(es, axis=0), v_ref[keys, lanes])
            o = jnp.where(low_p, o2[0:GRID_W] * inv[0:GRID_W], o2[GRID_W:] * inv[GRID_W:])
            out_ref[qrows, lanes] = (o * _silu(gate[qrows, lanes])).astype(BF16)


def _na_bias_table(rpb):
    c = jnp.arange(GRID_W, dtype=jnp.int32)
    cs = jnp.clip(c - NA_COLS // 2, 0, GRID_W - NA_COLS)
    cp = jnp.arange(GRID_W, dtype=jnp.int32)
    valid = (cp[None, :] >= cs[:, None]) & (cp[None, :] < cs[:, None] + NA_COLS)
    col_off = cp[None, :] - c[:, None] + NA_COLS - 1
    offs = jnp.arange(2 * NA_COLS - 1, dtype=jnp.int32)
    pick = ((col_off[None] == offs[:, None, None]) & valid[None]).astype(F32)
    t = jnp.einsum("hro,ocp->hcrp", rpb.astype(F32), pick, precision=lax.Precision.HIGHEST)
    t = jnp.where(valid[None, :, None, :], t * LOG2E, NEG_BIAS)
    tbl = jnp.stack([t[:, :, NA_ROWS - 1 - di:2 * NA_ROWS - 1 - di, :] for di in range(NA_ROWS)])
    return tbl.reshape(NA_ROWS, NA_HEADS // 2, 2 * GRID_W, NA_ROWS * GRID_W)


def _na_mixer(z3, bias_tbl, q_g, k_g, ones_bd, *, first_block):
    b, s, _ = z3.shape
    w = GROUP_W
    n_rows = s // GRID_W
    assert n_rows >= NA_ROWS and n_rows % NA_ROWS_PER_STEP == 0
    fb = first_block
    tq = NA_ROWS_PER_STEP * GRID_W
    const = lambda a: pl.BlockSpec(a.shape, lambda i, r, n=a.ndim: (0,) * n)
    return pl.pallas_call(
        functools.partial(_na_kernel, n_rows=n_rows, rows_per_step=NA_ROWS_PER_STEP),
        grid=(b, n_rows // NA_ROWS_PER_STEP),
        in_specs=[
            pl.BlockSpec((None, tq, w), lambda i, r: (i, r, fb)),
            pl.BlockSpec((None, s, w), lambda i, r: (i, 0, fb + 1)),
            pl.BlockSpec((None, s, w), lambda i, r: (i, 0, fb + 2)),
            pl.BlockSpec((None, tq, w), lambda i, r: (i, r, fb + 3)),
            const(bias_tbl), const(q_g), const(k_g), const(ones_bd),
        ],
        out_specs=pl.BlockSpec((None, tq, w), lambda i, r: (i, r, 0)),
        out_shape=jax.ShapeDtypeStruct((b, s, w), BF16),
        scratch_shapes=[pltpu.VMEM((s, w), BF16)],
        compiler_params=_cparams("parallel", "arbitrary"),
        name="na_mixer",
    )(z3, z3, z3, z3, bias_tbl, q_g, k_g, ones_bd)


def _gla_kernel(q_ref, k_ref, v_ref, gate_ref, lr_ref, a2f_ref, abf_ref, a2b_ref, abb_ref,
                tri_ref, bd_ref, og_ref, out_ref,
                bf_ref, bb_ref, of_ref, ob_ref, stf_ref, stb_ref):
    s = q_ref.shape[0]
    dk_all = q_ref.shape[1]
    dv_all = v_ref.shape[1]
    c = GLA_CHUNK
    n_chunks = s // c
    rows = tri_ref.shape[1]

    def decay(i, carry):
        sl = pl.ds(pl.multiple_of(i * rows, rows), rows)
        lr = lr_ref[sl, :]
        lr_hi = lr.astype(BF16)
        lr_lo = (lr - lr_hi.astype(F32)).astype(BF16)
        for a2_ref, ab_ref, b_ref, tri in ((a2f_ref, abf_ref, bf_ref, tri_ref[0]),
                                          (a2b_ref, abb_ref, bb_ref, tri_ref[1])):
            z = (_dot(lr_hi, a2_ref[0]) + _dot(lr_lo, a2_ref[0]) + _dot(lr_hi, a2_ref[1])) + ab_ref[...]
            g = (jnp.minimum(z, 0.0) - jnp.log1p(jnp.exp(-jnp.abs(z)))) * (1.0 / GLA_TAU)
            g_hi, g_mid, g_lo = _split3(g)
            b_ref[sl, :] = _dot(tri, g_hi) + _dot(tri, g_mid) + _dot(tri, g_lo)
        return carry
    lax.fori_loop(0, s // rows, decay, 0, unroll=2)

    stf_ref[...] = jnp.zeros_like(stf_ref)
    stb_ref[...] = jnp.zeros_like(stb_ref)

    lane = lax.broadcasted_iota(jnp.int32, (1, dk_all), 1)
    head_masks = [(lane // GLA_DK) == h for h in range(GLA_HEADS)]
    ri = lax.broadcasted_iota(jnp.int32, (c, c), 0)
    ci = lax.broadcasted_iota(jnp.int32, (c, c), 1)
    causal = (ri >= ci, ri <= ci)

    def one_chunk(ch, forward):
        b_ref, st_ref, o_ref = (bf_ref, stf_ref, of_ref) if forward else (bb_ref, stb_ref, ob_ref)
        sl = pl.ds(pl.multiple_of(ch * c, c), c)
        b = b_ref[sl, :]
        b_last = b[c - 1:c, :] if forward else b[0:1, :]
        q = q_ref[sl, :].astype(F32)
        k = k_ref[sl, :].astype(F32)
        v = v_ref[sl, :]
        qd = (q * ((GLA_DK ** -0.5) * jnp.exp(b))).astype(BF16)
        ki = (k * jnp.exp(-b)).astype(BF16)
        kd = (k * jnp.exp(b_last - b)).astype(BF16)
        zero = jnp.zeros_like(qd)
        q_stack = jnp.concatenate([jnp.where(hm, qd, zero) for hm in head_masks], axis=0)
        attn = _dot_nt(q_stack, ki)
        mask = causal[0] if forward else causal[1]
        state = st_ref[...]
        o = _dot_nt(qd, state.astype(BF16))
        intra = []
        for h in range(GLA_HEADS):
            a_h = jnp.where(mask, attn[h * c:(h + 1) * c, :], 0.0).astype(BF16)
            intra.append(_dot(a_h, v[:, h * GLA_DV:(h + 1) * GLA_DV]))
        o_ref[sl, :] = o + jnp.concatenate(intra, axis=-1)
        st_ref[...] = state * jnp.exp(b_last) + _dot_tn(v, kd) * bd_ref[...]

    def step(t, carry):
        one_chunk(t, True)
        one_chunk(n_chunks - 1 - t, False)
        return carry
    lax.fori_loop(0, n_chunks, step, 0, unroll=2)

    def finish(i, carry):
        sl = pl.ds(pl.multiple_of(i * rows, rows), rows)
        o = of_ref[sl, :] + ob_ref[sl, :]
        outs = []
        for h in range(GLA_HEADS):
            oh = o[:, h * GLA_DV:(h + 1) * GLA_DV]
            ms = jnp.mean(oh * oh, axis=-1, keepdims=True)
            outs.append(oh * lax.rsqrt(ms + EPS))
        y = jnp.concatenate(outs, axis=-1) * og_ref[...]
        gate = gate_ref[sl, :].astype(F32)
        out_ref[sl, :] = (y * _silu(gate)).astype(BF16)
        return carry
    lax.fori_loop(0, s // rows, finish, 0, unroll=2)


def _gla_mixer(z3, lr3, a2f, abf, a2b, abb, tri, bd_mask, o_g, *, q_block, v_block):
    b, s, _ = z3.shape
    dk_all = GLA_HEADS * GLA_DK
    dv_all = GLA_HEADS * GLA_DV
    full = lambda a: pl.BlockSpec(a.shape, lambda i, n=a.ndim: (0,) * n)
    return pl.pallas_call(
        _gla_kernel,
        grid=(b,),
        in_specs=[
            pl.BlockSpec((None, s, dk_all), lambda i: (i, 0, q_block)),
            pl.BlockSpec((None, s, dk_all), lambda i: (i, 0, q_block + 1)),
            pl.BlockSpec((None, s, dv_all), lambda i: (i, 0, v_block)),
            pl.BlockSpec((None, s, dv_all), lambda i: (i, 0, v_block + 1)),
            pl.BlockSpec((None, s, LANES), lambda i: (i, 0, 0)),
            full(a2f), full(abf), full(a2b), full(abb), full(tri), full(bd_mask), full(o_g),
        ],
        out_specs=pl.BlockSpec((None, s, dv_all), lambda i: (i, 0, 0)),
        out_shape=jax.ShapeDtypeStruct((b, s, dv_all), BF16),
        scratch_shapes=[
            pltpu.VMEM((s, dk_all), F32), pltpu.VMEM((s, dk_all), F32),
            pltpu.VMEM((s, dv_all), F32), pltpu.VMEM((s, dv_all), F32),
            pltpu.VMEM((dv_all, dk_all), F32), pltpu.VMEM((dv_all, dk_all), F32),
        ],
        compiler_params=_cparams("parallel"),
        name="gla_mixer",
    )(z3, z3, z3, z3, lr3, a2f, abf, a2b, abb, tri, bd_mask, o_g)


def _outproj_kernel(x_ref, ya_ref, yb_ref, yc_ref, yd_ref, w_ref, o_ref):
    y = jnp.concatenate([ya_ref[...], yb_ref[...], yc_ref[...], yd_ref[...]], axis=-1)
    o_ref[...] = x_ref[...] + _dot(y, w_ref[...])


def _outproj(x2d, ys, w_out, *, tm):
    m, d = x2d.shape
    gw = ys[0].shape[1]
    yspec = pl.BlockSpec((tm, gw), lambda i: (i, 0))
    return pl.pallas_call(
        _outproj_kernel,
        grid=(m // tm,),
        in_specs=[pl.BlockSpec((tm, d), lambda i: (i, 0)), yspec, yspec, yspec, yspec,
                  pl.BlockSpec(w_out.shape, lambda i: (0, 0))],
        out_specs=pl.BlockSpec((tm, d), lambda i: (i, 0)),
        out_shape=jax.ShapeDtypeStruct((m, d), F32),
        compiler_params=_cparams("parallel"),
        name="outproj",
    )(x2d, *ys, w_out)


def _gla_constants():
    rows = 4 * GLA_CHUNK
    i = jnp.arange(rows)
    same = (i[:, None] // GLA_CHUNK) == (i[None, :] // GLA_CHUNK)
    lower = same & (i[:, None] >= i[None, :])
    upper = same & (i[:, None] <= i[None, :])
    tri = jnp.stack([lower, upper]).astype(BF16)
    rv = jnp.arange(GLA_HEADS * GLA_DV) // GLA_DV
    rk = jnp.arange(GLA_HEADS * GLA_DK) // GLA_DK
    bd_mask = (rv[:, None] == rk[None, :]).astype(F32)
    return tri, bd_mask


def _pad_rank(a2, first_row):
    full = jnp.zeros((LANES, a2.shape[1]), F32).at[first_row:first_row + a2.shape[0]].set(a2.astype(F32))
    hi = full.astype(BF16)
    lo = (full - hi.astype(F32)).astype(BF16)
    return jnp.stack([hi, lo])


def _layer(x, norm_g, w_in, conv_w, conv_b, conv_ln_g, conv_ln_b, na_q_g, na_k_g, na_rpb,
           gla_a2_f, gla_ab_f, gla_a2_b, gla_ab_b, gla_o_g, pool_w, pool_scale, w_out, consts):
    b, s, d = x.shape
    m = b * s
    tri, bd_mask, ones_bd = consts
    lr0 = 7 * GROUP_W + 2 * GLA_HEADS * GLA_DK + 2 * GROUP_W
    lr1 = lr0 + 2 * GLA_RANK
    w_main = jnp.concatenate([w_in[:, :lr0], w_in[:, lr1:]], axis=1).astype(BF16)
    w_lr = jnp.zeros((d, LANES), F32).at[:, :2 * GLA_RANK].set(w_in[:, lr0:lr1]).astype(BF16)

    x2d = x.reshape(m, d)
    z, lr = _inproj(x2d, norm_g.reshape(1, d), w_main, w_lr, **INPROJ_TILES)
    z3 = z.reshape(b, s, z.shape[1])
    lr3 = lr.reshape(b, s, LANES)

    row = lambda a: a.reshape(1, -1).astype(F32)
    conv_w8 = jnp.repeat(conv_w.astype(F32), SUBLANES, axis=0)
    y_a = _conv_mixer(z3, conv_w8, row(conv_b), row(conv_ln_g), row(conv_ln_b))
    y_b = _na_mixer(z3, _na_bias_table(na_rpb), row(na_q_g), row(na_k_g), ones_bd, first_block=3)
    y_c = _gla_mixer(z3, lr3, _pad_rank(gla_a2_f, 0), row(gla_ab_f), _pad_rank(gla_a2_b, GLA_RANK),
                     row(gla_ab_b), tri, bd_mask, row(gla_o_g), q_block=14, v_block=8)
    y_d = _pool_mixer(z3, pool_w.astype(BF16), row(pool_scale), first_block=10)

    ys = [y.reshape(m, GROUP_W) for y in (y_a, y_b, y_c, y_d)]
    out = _outproj(x2d, ys, w_out.astype(BF16), **OUTPROJ_TILES)
    return out.reshape(b, s, d)


def kernel(x, norm_g, w_in, conv_w, conv_b, conv_ln_g, conv_ln_b, na_q_g, na_k_g, na_rpb, gla_a2_f,
           gla_ab_f, gla_a2_b, gla_ab_b, gla_o_g, pool_w, pool_scale, w_out):
    tri, bd_mask = _gla_constants()
    hq = jnp.arange(GROUP_W) // NA_HEAD_DIM
    ones_bd = (hq[:, None] == hq[None, :]).astype(BF16)
    consts = (tri, bd_mask, ones_bd)
    params = (norm_g, w_in, conv_w, conv_b, conv_ln_g, conv_ln_b, na_q_g, na_k_g, na_rpb, gla_a2_f,
              gla_ab_f, gla_a2_b, gla_ab_b, gla_o_g, pool_w, pool_scale, w_out)
    for l in range(norm_g.shape[0]):
        x = _layer(x, *[p[l] for p in params], consts)
    return x
```

```python
import functools

import jax
import jax.numpy as jnp
from jax import lax
from jax.experimental import pallas as pl
from jax.experimental.pallas import tpu as pltpu

F32 = jnp.float32
BF16 = jnp.bfloat16

EPS = 1e-6
GRID_W = 64
GROUP_W = 512
CONV_K = 31
NA_HEADS = 8
NA_HEAD_DIM = 64
NA_ROWS = 8
NA_COLS = 16
GLA_HEADS = 4
GLA_DK = 64
GLA_DV = 128
GLA_RANK = 16
GLA_TAU = 16.0
GLA_CHUNK = 64
POOL_WINDOWS = (2, 4, 8, 16)
POOL_CG = 128
LANES = 128
SUBLANES = 8
NEG_BIAS = -1e30
NA_ROWS_PER_STEP = 8
NA_SOFTMAX_ROWS = 32
LOG2E = 1.4426950408889634

VMEM_LIMIT = 56 * 1024 * 1024
INPROJ_TILES = dict(tm=1024, tn=2048)
INPROJ_NORM_ROWS = 256
OUTPROJ_TILES = dict(tm=512)


def _cparams(*sem):
    return pltpu.CompilerParams(dimension_semantics=sem, vmem_limit_bytes=VMEM_LIMIT)


def _silu(x):
    return x * jax.nn.sigmoid(x)


def _dot(a, b):
    return jnp.dot(a, b, preferred_element_type=F32)


def _dot_nt(a, b):
    return lax.dot_general(a, b, (((1,), (1,)), ((), ())), preferred_element_type=F32)


def _dot_tn(a, b):
    return lax.dot_general(a, b, (((0,), (0,)), ((), ())), preferred_element_type=F32)


def _split3(x):
    hi = x.astype(BF16)
    r1 = x - hi.astype(F32)
    mid = r1.astype(BF16)
    lo = (r1 - mid.astype(F32)).astype(BF16)
    return hi, mid, lo


def _inproj_kernel(x_ref, g_ref, w_ref, wlr_ref, z_ref, lr_ref, h_ref, *, rows_per_step):
    j = pl.program_id(1)
    tm = x_ref.shape[0]

    @pl.when(j == 0)
    def _():
        for c in range(tm // rows_per_step):
            r = slice(c * rows_per_step, (c + 1) * rows_per_step)
            x = x_ref[r, :]
            ms = jnp.mean(x * x, axis=-1, keepdims=True)
            h = (x * lax.rsqrt(ms + EPS) * g_ref[...]).astype(BF16)
            h_ref[r, :] = h
            lr_ref[r, :] = _dot(h, wlr_ref[...])
            z_ref[r, :] = _dot(h, w_ref[...]).astype(BF16)

    @pl.when(j > 0)
    def _():
        z_ref[...] = _dot(h_ref[...], w_ref[...]).astype(BF16)


def _inproj(x2d, norm_g, w_main, w_lr, *, tm, tn):
    m, d = x2d.shape
    n = w_main.shape[1]
    assert m % tm == 0 and n % tn == 0
    return pl.pallas_call(
        functools.partial(_inproj_kernel, rows_per_step=INPROJ_NORM_ROWS),
        grid=(m // tm, n // tn),
        in_specs=[
            pl.BlockSpec((tm, d), lambda i, j: (i, 0)),
            pl.BlockSpec((1, d), lambda i, j: (0, 0)),
            pl.BlockSpec((d, tn), lambda i, j: (0, j)),
            pl.BlockSpec((d, LANES), lambda i, j: (0, 0)),
        ],
        out_specs=[
            pl.BlockSpec((tm, tn), lambda i, j: (i, j)),
            pl.BlockSpec((tm, LANES), lambda i, j: (i, 0)),
        ],
        out_shape=[
            jax.ShapeDtypeStruct((m, n), BF16),
            jax.ShapeDtypeStruct((m, LANES), F32),
        ],
        scratch_shapes=[pltpu.VMEM((tm, d), BF16)],
        compiler_params=_cparams("parallel", "arbitrary"),
        name="inproj",
    )(x2d, norm_g, w_main, w_lr)


CONV_PAD = 16


def _conv_kernel(val_ref, glu_ref, gate_ref, cw_ref, cb_ref, lg_ref, lb_ref, out_ref, u_ref):
    s = val_ref.shape[0]
    w = val_ref.shape[1]
    fill_rows = 128
    tile = 32

    u_ref[0:CONV_PAD, :] = jnp.zeros((CONV_PAD, w), F32)
    u_ref[CONV_PAD + s:CONV_PAD + s + CONV_PAD, :] = jnp.zeros((CONV_PAD, w), F32)

    def fill(i, c):
        base = pl.multiple_of(i * fill_rows, fill_rows)
        v = val_ref[pl.ds(base, fill_rows), :].astype(F32)
        g = glu_ref[pl.ds(base, fill_rows), :].astype(F32)
        u_ref[pl.ds(base + CONV_PAD, fill_rows), :] = v * jax.nn.sigmoid(g)
        return c
    lax.fori_loop(0, s // fill_rows, fill, 0)

    win_rows = tile + 2 * CONV_PAD

    def conv(i, c):
        base = pl.multiple_of(i * tile, tile)
        acc = jnp.zeros((tile, w), F32)
        for r in range(SUBLANES):
            offs = [o for o in range(CONV_PAD - CONV_K // 2, CONV_PAD + CONV_K // 2 + 1) if o % SUBLANES == r]
            win = u_ref[pl.ds(base, win_rows), :]
            if r:
                win = pltpu.roll(win, win_rows - r, axis=0)
            for o in offs:
                k = o - (CONV_PAD - CONV_K // 2)
                wk = pltpu.repeat(cw_ref[k * SUBLANES:(k + 1) * SUBLANES, :], tile // SUBLANES, axis=0)
                acc = acc + win[o - r:o - r + tile, :] * wk
        y = acc + cb_ref[...]
        mu = jnp.mean(y, axis=-1, keepdims=True)
        d = y - mu
        var = jnp.mean(d * d, axis=-1, keepdims=True)
        yn = d * lax.rsqrt(var + EPS) * lg_ref[...] + lb_ref[...]
        gate = gate_ref[pl.ds(base, tile), :].astype(F32)
        out_ref[pl.ds(base, tile), :] = (_silu(yn) * _silu(gate)).astype(BF16)
        return c
    lax.fori_loop(0, s // tile, conv, 0, unroll=4)


def _conv_mixer(z3, conv_w, conv_b, ln_g, ln_b):
    b, s, _ = z3.shape
    w = GROUP_W
    col = lambda j: pl.BlockSpec((None, s, w), lambda i, j=j: (i, 0, j))
    vec = lambda r: pl.BlockSpec((r, w), lambda i: (0, 0))
    return pl.pallas_call(
        _conv_kernel,
        grid=(b,),
        in_specs=[col(0), col(1), col(2), vec(CONV_K * SUBLANES), vec(1), vec(1), vec(1)],
        out_specs=pl.BlockSpec((None, s, w), lambda i: (i, 0, 0)),
        out_shape=jax.ShapeDtypeStruct((b, s, w), BF16),
        scratch_shapes=[pltpu.VMEM((s + 2 * CONV_PAD, w), F32)],
        compiler_params=_cparams("parallel"),
        name="conv_mixer",
    )(z3, z3, z3, conv_w, conv_b, ln_g, ln_b)


POOL_PAD = 8


def _pool_kernel(val_ref, gate_ref, pw_ref, ps_ref, out_ref, u_ref):
    s = val_ref.shape[0]
    w = val_ref.shape[1]
    tile = 128

    u_ref[0:POOL_PAD, :] = jnp.zeros((POOL_PAD, w), F32)
    u_ref[POOL_PAD + s:POOL_PAD + s + POOL_PAD, :] = jnp.zeros((POOL_PAD, w), F32)

    def fill(i, c):
        base = pl.multiple_of(i * tile, tile)
        u_ref[pl.ds(base + POOL_PAD, tile), :] = val_ref[pl.ds(base, tile), :].astype(F32)
        return c
    lax.fori_loop(0, s // tile, fill, 0)

    win_rows = tile + 2 * POOL_PAD

    def shifted(a, o):
        return a if o == 0 else pltpu.roll(a, win_rows - o, axis=0)

    def pool(i, c):
        base = pl.multiple_of(i * tile, tile)
        t = base + lax.broadcasted_iota(jnp.int32, (tile, 1), 0)
        ys = []
        for g, win in enumerate(POOL_WINDOWS):
            half = win // 2
            lanes = slice(g * POOL_CG, (g + 1) * POOL_CG)
            x = u_ref[pl.ds(base, win_rows), lanes]
            sm, span = x, 1
            while span < win:
                sm = sm + shifted(sm, span)
                span *= 2
            acc = shifted(sm, POOL_PAD - half)[0:tile]
            cnt = jnp.minimum(t + half, s) - jnp.maximum(t - half, 0)
            centre = x[POOL_PAD:POOL_PAD + tile]
            diff = acc / cnt.astype(F32) - centre
            ys.append(_dot(diff.astype(BF16), pw_ref[g]))
        y = jnp.concatenate(ys, axis=-1) * ps_ref[...]
        gate = gate_ref[pl.ds(base, tile), :].astype(F32)
        out_ref[pl.ds(base, tile), :] = (y * _silu(gate)).astype(BF16)
        return c
    lax.fori_loop(0, s // tile, pool, 0, unroll=2)


def _pool_mixer(z3, pool_w, pool_scale, *, first_block):
    b, s, _ = z3.shape
    w = GROUP_W
    col = lambda j: pl.BlockSpec((None, s, w), lambda i, j=j: (i, 0, j))
    return pl.pallas_call(
        _pool_kernel,
        grid=(b,),
        in_specs=[
            col(first_block), col(first_block + 1),
            pl.BlockSpec(pool_w.shape, lambda i: (0, 0, 0)),
            pl.BlockSpec((1, w), lambda i: (0, 0)),
        ],
        out_specs=pl.BlockSpec((None, s, w), lambda i: (i, 0, 0)),
        out_shape=jax.ShapeDtypeStruct((b, s, w), BF16),
        scratch_shapes=[pltpu.VMEM((s + 2 * POOL_PAD, w), F32)],
        compiler_params=_cparams("parallel"),
        name="pool_mixer",
    )(z3, z3, pool_w, pool_scale)


def _head_rms(x, ones_bd, gain):
    ss = _dot((x * x).astype(BF16), ones_bd)
    return x * lax.rsqrt(ss * (1.0 / NA_HEAD_DIM) + EPS) * gain


def _na_kernel(q_ref, k_ref, v_ref, gate_ref, bias_ref, qg_ref, kg_ref, ones_ref, out_ref,
               kn_ref, *, n_rows, rows_per_step):
    r = pl.program_id(1)
    s = k_ref.shape[0]
    n_keys = NA_ROWS * GRID_W

    @pl.when(r == 0)
    def _():
        rows = 256

        def prep(i, c):
            sl = pl.ds(pl.multiple_of(i * rows, rows), rows)
            k = k_ref[sl, :].astype(F32)
            kn_ref[sl, :] = _head_rms(k, ones_ref[...], kg_ref[...]).astype(BF16)
            return c
        lax.fori_loop(0, s // rows, prep, 0)

    q = q_ref[...].astype(F32)
    qn = (_head_rms(q, ones_ref[...], qg_ref[...]) * (LOG2E * NA_HEAD_DIM ** -0.5)).astype(BF16)
    lane_p = lax.broadcasted_iota(jnp.int32, (1, LANES), 1)
    low_p = lane_p < NA_HEAD_DIM
    gate = gate_ref[...].astype(F32)

    for qi in range(rows_per_step):
        row = r * rows_per_step + qi
        row_start = jnp.clip(row - NA_ROWS // 2, 0, n_rows - NA_ROWS)
        keys = pl.ds(pl.multiple_of(row_start * GRID_W, GRID_W), n_keys)
        qrows = slice(qi * GRID_W, (qi + 1) * GRID_W)
        scores = []
        for p in range(NA_HEADS // 2):
            lanes = slice(p * LANES, (p + 1) * LANES)
            qp = qn[qrows, lanes]
            zero = jnp.zeros_like(qp)
            qm = jnp.concatenate([jnp.where(low_p, qp, zero), jnp.where(low_p, zero, qp)], axis=0)
            scores.append(_dot_nt(qm, kn_ref[keys, lanes]))
        for p in range(NA_HEADS // 2):
            lanes = slice(p * LANES, (p + 1) * LANES)
            es, invs = [], []
            for g in range(2 * GRID_W // NA_SOFTMAX_ROWS):
                rows = slice(g * NA_SOFTMAX_ROWS, (g + 1) * NA_SOFTMAX_ROWS)
                sc = scores[p][rows] + bias_ref[row - row_start, p, rows, :]
                m = jnp.max(sc, axis=-1, keepdims=True)
                e = jnp.exp2(sc - m)
                invs.append(1.0 / jnp.sum(e, axis=-1, keepdims=True))
                es.append(e.astype(BF16))
            inv = jnp.concatenate(invs, axis=0)
            o2 = _dot(jnp.concatenate(es, axis=0), v_ref[keys, lanes])
            o = jnp.where(low_p, o2[0:GRID_W] * inv[0:GRID_W], o2[GRID_W:] * inv[GRID_W:])
            out_ref[qrows, lanes] = (o * _silu(gate[qrows, lanes])).astype(BF16)


def _na_bias_table(rpb):
    c = jnp.arange(GRID_W, dtype=jnp.int32)
    cs = jnp.clip(c - NA_COLS // 2, 0, GRID_W - NA_COLS)
    cp = jnp.arange(GRID_W, dtype=jnp.int32)
    valid = (cp[None, :] >= cs[:, None]) & (cp[None, :] < cs[:, None] + NA_COLS)
    col_off = cp[None, :] - c[:, None] + NA_COLS - 1
    offs = jnp.arange(2 * NA_COLS - 1, dtype=jnp.int32)
    pick = ((col_off[None] == offs[:, None, None]) & valid[None]).astype(F32)
    t = jnp.einsum("hro,ocp->hcrp", rpb.astype(F32), pick, precision=lax.Precision.HIGHEST)
    t = jnp.where(valid[None, :, None, :], t * LOG2E, NEG_BIAS)
    tbl = jnp.stack([t[:, :, NA_ROWS - 1 - di:2 * NA_ROWS - 1 - di, :] for di in range(NA_ROWS)])
    return tbl.reshape(NA_ROWS, NA_HEADS // 2, 2 * GRID_W, NA_ROWS * GRID_W)


def _na_mixer(z3, bias_tbl, q_g, k_g, ones_bd, *, first_block):
    b, s, _ = z3.shape
    w = GROUP_W
    n_rows = s // GRID_W
    assert n_rows >= NA_ROWS and n_rows % NA_ROWS_PER_STEP == 0
    fb = first_block
    tq = NA_ROWS_PER_STEP * GRID_W
    const = lambda a: pl.BlockSpec(a.shape, lambda i, r, n=a.ndim: (0,) * n)
    return pl.pallas_call(
        functools.partial(_na_kernel, n_rows=n_rows, rows_per_step=NA_ROWS_PER_STEP),
        grid=(b, n_rows // NA_ROWS_PER_STEP),
        in_specs=[
            pl.BlockSpec((None, tq, w), lambda i, r: (i, r, fb)),
            pl.BlockSpec((None, s, w), lambda i, r: (i, 0, fb + 1)),
            pl.BlockSpec((None, s, w), lambda i, r: (i, 0, fb + 2)),
            pl.BlockSpec((None, tq, w), lambda i, r: (i, r, fb + 3)),
            const(bias_tbl), const(q_g), const(k_g), const(ones_bd),
        ],
        out_specs=pl.BlockSpec((None, tq, w), lambda i, r: (i, r, 0)),
        out_shape=jax.ShapeDtypeStruct((b, s, w), BF16),
        scratch_shapes=[pltpu.VMEM((s, w), BF16)],
        compiler_params=_cparams("parallel", "arbitrary"),
        name="na_mixer",
    )(z3, z3, z3, z3, bias_tbl, q_g, k_g, ones_bd)


def _gla_kernel(q_ref, k_ref, v_ref, gate_ref, lr_ref, a2f_ref, abf_ref, a2b_ref, abb_ref,
                tri_ref, bd_ref, og_ref, out_ref,
                bf_ref, bb_ref, of_ref, ob_ref, stf_ref, stb_ref):
    s = q_ref.shape[0]
    dk_all = q_ref.shape[1]
    dv_all = v_ref.shape[1]
    c = GLA_CHUNK
    n_chunks = s // c
    rows = tri_ref.shape[1]

    def decay(i, carry):
        sl = pl.ds(pl.multiple_of(i * rows, rows), rows)
        lr = lr_ref[sl, :]
        lr_hi = lr.astype(BF16)
        lr_lo = (lr - lr_hi.astype(F32)).astype(BF16)
        for a2_ref, ab_ref, b_ref, tri in ((a2f_ref, abf_ref, bf_ref, tri_ref[0]),
                                          (a2b_ref, abb_ref, bb_ref, tri_ref[1])):
            z = (_dot(lr_hi, a2_ref[0]) + _dot(lr_lo, a2_ref[0]) + _dot(lr_hi, a2_ref[1])) + ab_ref[...]
            g = (jnp.minimum(z, 0.0) - jnp.log1p(jnp.exp(-jnp.abs(z)))) * (1.0 / GLA_TAU)
            g_hi, g_mid, g_lo = _split3(g)
            b_ref[sl, :] = _dot(tri, g_hi) + _dot(tri, g_mid) + _dot(tri, g_lo)
        return carry
    lax.fori_loop(0, s // rows, decay, 0, unroll=2)

    stf_ref[...] = jnp.zeros_like(stf_ref)
    stb_ref[...] = jnp.zeros_like(stb_ref)

    lane = lax.broadcasted_iota(jnp.int32, (1, dk_all), 1)
    head_masks = [(lane // GLA_DK) == h for h in range(GLA_HEADS)]
    ri = lax.broadcasted_iota(jnp.int32, (c, c), 0)
    ci = lax.broadcasted_iota(jnp.int32, (c, c), 1)
    causal = (ri >= ci, ri <= ci)

    def one_chunk(ch, forward):
        b_ref, st_ref, o_ref = (bf_ref, stf_ref, of_ref) if forward else (bb_ref, stb_ref, ob_ref)
        sl = pl.ds(pl.multiple_of(ch * c, c), c)
        b = b_ref[sl, :]
        b_last = b[c - 1:c, :] if forward else b[0:1, :]
        q = q_ref[sl, :].astype(F32)
        k = k_ref[sl, :].astype(F32)
        v = v_ref[sl, :]
        qd = (q * ((GLA_DK ** -0.5) * jnp.exp(b))).astype(BF16)
        ki = (k * jnp.exp(-b)).astype(BF16)
        kd = (k * jnp.exp(b_last - b)).astype(BF16)
        zero = jnp.zeros_like(qd)
        q_stack = jnp.concatenate([jnp.where(hm, qd, zero) for hm in head_masks], axis=0)
        attn = _dot_nt(q_stack, ki)
        mask = causal[0] if forward else causal[1]
        state = st_ref[...]
        o = _dot_nt(qd, state.astype(BF16))
        intra = []
        for h in range(GLA_HEADS):
            a_h = jnp.where(mask, attn[h * c:(h + 1) * c, :], 0.0).astype(BF16)
            intra.append(_dot(a_h, v[:, h * GLA_DV:(h + 1) * GLA_DV]))
        o_ref[sl, :] = o + jnp.concatenate(intra, axis=-1)
        st_ref[...] = state * jnp.exp(b_last) + _dot_tn(v, kd) * bd_ref[...]

    def step(t, carry):
        one_chunk(t, True)
        one_chunk(n_chunks - 1 - t, False)
        return carry
    lax.fori_loop(0, n_chunks, step, 0, unroll=2)

    def finish(i, carry):
        sl = pl.ds(pl.multiple_of(i * rows, rows), rows)
        o = of_ref[sl, :] + ob_ref[sl, :]
        outs = []
        for h in range(GLA_HEADS):
            oh = o[:, h * GLA_DV:(h + 1) * GLA_DV]
            ms = jnp.mean(oh * oh, axis=-1, keepdims=True)
            outs.append(oh * lax.rsqrt(ms + EPS))
        y = jnp.concatenate(outs, axis=-1) * og_ref[...]
        gate = gate_ref[sl, :].astype(F32)
        out_ref[sl, :] = (y * _silu(gate)).astype(BF16)
        return carry
    lax.fori_loop(0, s // rows, finish, 0, unroll=2)


def _gla_mixer(z3, lr3, a2f, abf, a2b, abb, tri, bd_mask, o_g, *, q_block, v_block):
    b, s, _ = z3.shape
    dk_all = GLA_HEADS * GLA_DK
    dv_all = GLA_HEADS * GLA_DV
    full = lambda a: pl.BlockSpec(a.shape, lambda i, n=a.ndim: (0,) * n)
    return pl.pallas_call(
        _gla_kernel,
        grid=(b,),
        in_specs=[
            pl.BlockSpec((None, s, dk_all), lambda i: (i, 0, q_block)),
            pl.BlockSpec((None, s, dk_all), lambda i: (i, 0, q_block + 1)),
            pl.BlockSpec((None, s, dv_all), lambda i: (i, 0, v_block)),
            pl.BlockSpec((None, s, dv_all), lambda i: (i, 0, v_block + 1)),
            pl.BlockSpec((None, s, LANES), lambda i: (i, 0, 0)),
            full(a2f), full(abf), full(a2b), full(abb), full(tri), full(bd_mask), full(o_g),
        ],
        out_specs=pl.BlockSpec((None, s, dv_all), lambda i: (i, 0, 0)),
        out_shape=jax.ShapeDtypeStruct((b, s, dv_all), BF16),
        scratch_shapes=[
            pltpu.VMEM((s, dk_all), F32), pltpu.VMEM((s, dk_all), F32),
            pltpu.VMEM((s, dv_all), F32), pltpu.VMEM((s, dv_all), F32),
            pltpu.VMEM((dv_all, dk_all), F32), pltpu.VMEM((dv_all, dk_all), F32),
        ],
        compiler_params=_cparams("parallel"),
        name="gla_mixer",
    )(z3, z3, z3, z3, lr3, a2f, abf, a2b, abb, tri, bd_mask, o_g)


def _outproj_kernel(x_ref, ya_ref, yb_ref, yc_ref, yd_ref, w_ref, o_ref):
    y = jnp.concatenate([ya_ref[...], yb_ref[...], yc_ref[...], yd_ref[...]], axis=-1)
    o_ref[...] = x_ref[...] + _dot(y, w_ref[...])


def _outproj(x2d, ys, w_out, *, tm):
    m, d = x2d.shape
    gw = ys[0].shape[1]
    yspec = pl.BlockSpec((tm, gw), lambda i: (i, 0))
    return pl.pallas_call(
        _outproj_kernel,
        grid=(m // tm,),
        in_specs=[pl.BlockSpec((tm, d), lambda i: (i, 0)), yspec, yspec, yspec, yspec,
                  pl.BlockSpec(w_out.shape, lambda i: (0, 0))],
        out_specs=pl.BlockSpec((tm, d), lambda i: (i, 0)),
        out_shape=jax.ShapeDtypeStruct((m, d), F32),
        compiler_params=_cparams("parallel"),
        name="outproj",
    )(x2d, *ys, w_out)


def _gla_constants():
    rows = 4 * GLA_CHUNK
    i = jnp.arange(rows)
    same = (i[:, None] // GLA_CHUNK) == (i[None, :] // GLA_CHUNK)
    lower = same & (i[:, None] >= i[None, :])
    upper = same & (i[:, None] <= i[None, :])
    tri = jnp.stack([lower, upper]).astype(BF16)
    rv = jnp.arange(GLA_HEADS * GLA_DV) // GLA_DV
    rk = jnp.arange(GLA_HEADS * GLA_DK) // GLA_DK
    bd_mask = (rv[:, None] == rk[None, :]).astype(F32)
    return tri, bd_mask


def _pad_rank(a2, first_row):
    full = jnp.zeros((LANES, a2.shape[1]), F32).at[first_row:first_row + a2.shape[0]].set(a2.astype(F32))
    hi = full.astype(BF16)
    lo = (full - hi.astype(F32)).astype(BF16)
    return jnp.stack([hi, lo])


def _layer(x, norm_g, w_in, conv_w, conv_b, conv_ln_g, conv_ln_b, na_q_g, na_k_g, na_rpb,
           gla_a2_f, gla_ab_f, gla_a2_b, gla_ab_b, gla_o_g, pool_w, pool_scale, w_out, consts):
    b, s, d = x.shape
    m = b * s
    tri, bd_mask, ones_bd = consts
    lr0 = 7 * GROUP_W + 2 * GLA_HEADS * GLA_DK + 2 * GROUP_W
    lr1 = lr0 + 2 * GLA_RANK
    w_main = jnp.concatenate([w_in[:, :lr0], w_in[:, lr1:]], axis=1).astype(BF16)
    w_lr = jnp.zeros((d, LANES), F32).at[:, :2 * GLA_RANK].set(w_in[:, lr0:lr1]).astype(BF16)

    x2d = x.reshape(m, d)
    z, lr = _inproj(x2d, norm_g.reshape(1, d), w_main, w_lr, **INPROJ_TILES)
    z3 = z.reshape(b, s, z.shape[1])
    lr3 = lr.reshape(b, s, LANES)

    row = lambda a: a.reshape(1, -1).astype(F32)
    conv_w8 = jnp.repeat(conv_w.astype(F32), SUBLANES, axis=0)
    y_a = _conv_mixer(z3, conv_w8, row(conv_b), row(conv_ln_g), row(conv_ln_b))
    y_b = _na_mixer(z3, _na_bias_table(na_rpb), row(na_q_g), row(na_k_g), ones_bd, first_block=3)
    y_c = _gla_mixer(z3, lr3, _pad_rank(gla_a2_f, 0), row(gla_ab_f), _pad_rank(gla_a2_b, GLA_RANK),
                     row(gla_ab_b), tri, bd_mask, row(gla_o_g), q_block=14, v_block=8)
    y_d = _pool_mixer(z3, pool_w.astype(BF16), row(pool_scale), first_block=10)

    ys = [y.reshape(m, GROUP_W) for y in (y_a, y_b, y_c, y_d)]
    out = _outproj(x2d, ys, w_out.astype(BF16), **OUTPROJ_TILES)
    return out.reshape(b, s, d)


def kernel(x, norm_g, w_in, conv_w, conv_b, conv_ln_g, conv_ln_b, na_q_g, na_k_g, na_rpb, gla_a2_f,
           gla_ab_f, gla_a2_b, gla_ab_b, gla_o_g, pool_w, pool_scale, w_out):
    tri, bd_mask = _gla_constants()
    hq = jnp.arange(GROUP_W) // NA_HEAD_DIM
    ones_bd = (hq[:, None] == hq[None, :]).astype(BF16)
    consts = (tri, bd_mask, ones_bd)
    params = (norm_g, w_in, conv_w, conv_b, conv_ln_g, conv_ln_b, na_q_g, na_k_g, na_rpb, gla_a2_f,
              gla_ab_f, gla_a2_b, gla_ab_b, gla_o_g, pool_w, pool_scale, w_out)
    for l in range(norm_g.shape[0]):
        x = _layer(x, *[p[l] for p in params], consts)
    return x
```

```python
import functools

import jax
import jax.numpy as jnp
from jax import lax
from jax.experimental import pallas as pl
from jax.experimental.pallas import tpu as pltpu

F32 = jnp.float32
BF16 = jnp.bfloat16

EPS = 1e-6
GRID_W = 64
GROUP_W = 512
CONV_K = 31
NA_HEADS = 8
NA_HEAD_DIM = 64
NA_ROWS = 8
NA_COLS = 16
GLA_HEADS = 4
GLA_DK = 64
GLA_DV = 128
GLA_RANK = 16
GLA_TAU = 16.0
GLA_CHUNK = 64
POOL_WINDOWS = (2, 4, 8, 16)
POOL_CG = 128
LANES = 128
SUBLANES = 8
NEG_BIAS = -1e30
NA_ROWS_PER_STEP = 16
NA_SOFTMAX_ROWS = 32
LOG2E = 1.4426950408889634

VMEM_LIMIT = 56 * 1024 * 1024
INPROJ_TILES = dict(tm=1024, tn=2048)
INPROJ_NORM_ROWS = 256
OUTPROJ_TILES = dict(tm=512)


def _cparams(*sem):
    return pltpu.CompilerParams(dimension_semantics=sem, vmem_limit_bytes=VMEM_LIMIT)


def _silu(x):
    return x * jax.nn.sigmoid(x)


def _dot(a, b):
    return jnp.dot(a, b, preferred_element_type=F32)


def _dot_nt(a, b):
    return lax.dot_general(a, b, (((1,), (1,)), ((), ())), preferred_element_type=F32)


def _dot_tn(a, b):
    return lax.dot_general(a, b, (((0,), (0,)), ((), ())), preferred_element_type=F32)


def _split3(x):
    hi = x.astype(BF16)
    r1 = x - hi.astype(F32)
    mid = r1.astype(BF16)
    lo = (r1 - mid.astype(F32)).astype(BF16)
    return hi, mid, lo


def _inproj_kernel(x_ref, g_ref, w_ref, wlr_ref, z_ref, lr_ref, h_ref, *, rows_per_step):
    j = pl.program_id(1)
    tm = x_ref.shape[0]

    @pl.when(j == 0)
    def _():
        for c in range(tm // rows_per_step):
            r = slice(c * rows_per_step, (c + 1) * rows_per_step)
            x = x_ref[r, :]
            ms = jnp.mean(x * x, axis=-1, keepdims=True)
            h = (x * lax.rsqrt(ms + EPS) * g_ref[...]).astype(BF16)
            h_ref[r, :] = h
            lr_ref[r, :] = _dot(h, wlr_ref[...])
            z_ref[r, :] = _dot(h, w_ref[...]).astype(BF16)

    @pl.when(j > 0)
    def _():
        z_ref[...] = _dot(h_ref[...], w_ref[...]).astype(BF16)


def _inproj(x2d, norm_g, w_main, w_lr, *, tm, tn, n=None, lr_block=0):
    m, d = x2d.shape
    n = w_main.shape[1] if n is None else n
    assert m % tm == 0 and n % tn == 0
    return pl.pallas_call(
        functools.partial(_inproj_kernel, rows_per_step=INPROJ_NORM_ROWS),
        grid=(m // tm, n // tn),
        in_specs=[
            pl.BlockSpec((tm, d), lambda i, j: (i, 0)),
            pl.BlockSpec((1, d), lambda i, j: (0, 0)),
            pl.BlockSpec((d, tn), lambda i, j: (0, j)),
            pl.BlockSpec((d, LANES), lambda i, j: (0, lr_block)),
        ],
        out_specs=[
            pl.BlockSpec((tm, tn), lambda i, j: (i, j)),
            pl.BlockSpec((tm, LANES), lambda i, j: (i, 0)),
        ],
        out_shape=[
            jax.ShapeDtypeStruct((m, n), BF16),
            jax.ShapeDtypeStruct((m, LANES), F32),
        ],
        scratch_shapes=[pltpu.VMEM((tm, d), BF16)],
        compiler_params=_cparams("parallel", "arbitrary"),
        name="inproj",
    )(x2d, norm_g, w_main, w_lr)


CONV_PAD = 16


def _conv_kernel(val_ref, glu_ref, gate_ref, cw_ref, cb_ref, lg_ref, lb_ref, out_ref, u_ref):
    s = val_ref.shape[0]
    w = val_ref.shape[1]
    fill_rows = 128
    tile = 32

    u_ref[0:CONV_PAD, :] = jnp.zeros((CONV_PAD, w), F32)
    u_ref[CONV_PAD + s:CONV_PAD + s + CONV_PAD, :] = jnp.zeros((CONV_PAD, w), F32)

    def fill(i, c):
        base = pl.multiple_of(i * fill_rows, fill_rows)
        v = val_ref[pl.ds(base, fill_rows), :].astype(F32)
        g = glu_ref[pl.ds(base, fill_rows), :].astype(F32)
        u_ref[pl.ds(base + CONV_PAD, fill_rows), :] = v * jax.nn.sigmoid(g)
        return c
    lax.fori_loop(0, s // fill_rows, fill, 0)

    win_rows = tile + 2 * CONV_PAD

    def conv(i, c):
        base = pl.multiple_of(i * tile, tile)
        acc = jnp.zeros((tile, w), F32)
        for r in range(SUBLANES):
            offs = [o for o in range(CONV_PAD - CONV_K // 2, CONV_PAD + CONV_K // 2 + 1) if o % SUBLANES == r]
            win = u_ref[pl.ds(base, win_rows), :]
            if r:
                win = pltpu.roll(win, win_rows - r, axis=0)
            for o in offs:
                k = o - (CONV_PAD - CONV_K // 2)
                wk = pltpu.repeat(cw_ref[k * SUBLANES:(k + 1) * SUBLANES, :], tile // SUBLANES, axis=0)
                acc = acc + win[o - r:o - r + tile, :] * wk
        y = acc + cb_ref[...]
        mu = jnp.mean(y, axis=-1, keepdims=True)
        d = y - mu
        var = jnp.mean(d * d, axis=-1, keepdims=True)
        yn = d * lax.rsqrt(var + EPS) * lg_ref[...] + lb_ref[...]
        gate = gate_ref[pl.ds(base, tile), :].astype(F32)
        out_ref[pl.ds(base, tile), :] = (_silu(yn) * _silu(gate)).astype(BF16)
        return c
    lax.fori_loop(0, s // tile, conv, 0, unroll=4)


def _conv_mixer(z3, conv_w, conv_b, ln_g, ln_b):
    b, s, _ = z3.shape
    w = GROUP_W
    col = lambda j: pl.BlockSpec((None, s, w), lambda i, j=j: (i, 0, j))
    vec = lambda r: pl.BlockSpec((r, w), lambda i: (0, 0))
    return pl.pallas_call(
        _conv_kernel,
        grid=(b,),
        in_specs=[col(0), col(1), col(2), vec(CONV_K * SUBLANES), vec(1), vec(1), vec(1)],
        out_specs=pl.BlockSpec((None, s, w), lambda i: (i, 0, 0)),
        out_shape=jax.ShapeDtypeStruct((b, s, w), BF16),
        scratch_shapes=[pltpu.VMEM((s + 2 * CONV_PAD, w), F32)],
        compiler_params=_cparams("parallel"),
        name="conv_mixer",
    )(z3, z3, z3, conv_w, conv_b, ln_g, ln_b)


POOL_PAD = 8


def _pool_kernel(val_ref, gate_ref, pw_ref, ps_ref, out_ref, u_ref):
    s = val_ref.shape[0]
    w = val_ref.shape[1]
    tile = 128

    u_ref[0:POOL_PAD, :] = jnp.zeros((POOL_PAD, w), F32)
    u_ref[POOL_PAD + s:POOL_PAD + s + POOL_PAD, :] = jnp.zeros((POOL_PAD, w), F32)

    def fill(i, c):
        base = pl.multiple_of(i * tile, tile)
        u_ref[pl.ds(base + POOL_PAD, tile), :] = val_ref[pl.ds(base, tile), :].astype(F32)
        return c
    lax.fori_loop(0, s // tile, fill, 0)

    win_rows = tile + 2 * POOL_PAD

    def shifted(a, o):
        return a if o == 0 else pltpu.roll(a, win_rows - o, axis=0)

    def pool(i, c):
        base = pl.multiple_of(i * tile, tile)
        t = base + lax.broadcasted_iota(jnp.int32, (tile, 1), 0)
        ys = []
        for g, win in enumerate(POOL_WINDOWS):
            half = win // 2
            lanes = slice(g * POOL_CG, (g + 1) * POOL_CG)
            x = u_ref[pl.ds(base, win_rows), lanes]
            sm, span = x, 1
            while span < win:
                sm = sm + shifted(sm, span)
                span *= 2
            acc = shifted(sm, POOL_PAD - half)[0:tile]
            cnt = jnp.minimum(t + half, s) - jnp.maximum(t - half, 0)
            centre = x[POOL_PAD:POOL_PAD + tile]
            diff = acc / cnt.astype(F32) - centre
            ys.append(_dot(diff.astype(BF16), pw_ref[g]))
        y = jnp.concatenate(ys, axis=-1) * ps_ref[...]
        gate = gate_ref[pl.ds(base, tile), :].astype(F32)
        out_ref[pl.ds(base, tile), :] = (y * _silu(gate)).astype(BF16)
        return c
    lax.fori_loop(0, s // tile, pool, 0, unroll=2)


def _pool_mixer(z3, pool_w, pool_scale, *, first_block):
    b, s, _ = z3.shape
    w = GROUP_W
    col = lambda j: pl.BlockSpec((None, s, w), lambda i, j=j: (i, 0, j))
    return pl.pallas_call(
        _pool_kernel,
        grid=(b,),
        in_specs=[
            col(first_block), col(first_block + 1),
            pl.BlockSpec(pool_w.shape, lambda i: (0, 0, 0)),
            pl.BlockSpec((1, w), lambda i: (0, 0)),
        ],
        out_specs=pl.BlockSpec((None, s, w), lambda i: (i, 0, 0)),
        out_shape=jax.ShapeDtypeStruct((b, s, w), BF16),
        scratch_shapes=[pltpu.VMEM((s + 2 * POOL_PAD, w), F32)],
        compiler_params=_cparams("parallel"),
        name="pool_mixer",
    )(z3, z3, pool_w, pool_scale)


def _head_rms(x, ones_bd, gain):
    ss = _dot((x * x).astype(BF16), ones_bd)
    return x * lax.rsqrt(ss * (1.0 / NA_HEAD_DIM) + EPS) * gain


def _na_kernel(q_ref, k_ref, v_ref, gate_ref, bias_ref, qg_ref, kg_ref, ones_ref, out_ref,
               kn_ref, *, n_rows, rows_per_step):
    r = pl.program_id(1)
    s = k_ref.shape[0]
    n_keys = NA_ROWS * GRID_W

    @pl.when(r == 0)
    def _():
        rows = 256

        def prep(i, c):
            sl = pl.ds(pl.multiple_of(i * rows, rows), rows)
            k = k_ref[sl, :].astype(F32)
            kn_ref[sl, :] = _head_rms(k, ones_ref[...], kg_ref[...]).astype(BF16)
            return c
        lax.fori_loop(0, s // rows, prep, 0)

    q = q_ref[...].astype(F32)
    qn = (_head_rms(q, ones_ref[...], qg_ref[...]) * (LOG2E * NA_HEAD_DIM ** -0.5)).astype(BF16)
    lane_p = lax.broadcasted_iota(jnp.int32, (1, LANES), 1)
    low_p = lane_p < NA_HEAD_DIM
    gate = gate_ref[...].astype(F32)

    for qi in range(rows_per_step):
        row = r * rows_per_step + qi
        row_start = jnp.clip(row - NA_ROWS // 2, 0, n_rows - NA_ROWS)
        keys = pl.ds(pl.multiple_of(row_start * GRID_W, GRID_W), n_keys)
        qrows = slice(qi * GRID_W, (qi + 1) * GRID_W)
        scores = []
        for p in range(NA_HEADS // 2):
            lanes = slice(p * LANES, (p + 1) * LANES)
            qp = qn[qrows, lanes]
            zero = jnp.zeros_like(qp)
            qm = jnp.concatenate([jnp.where(low_p, qp, zero), jnp.where(low_p, zero, qp)], axis=0)
            scores.append(_dot_nt(qm, kn_ref[keys, lanes]))
        for p in range(NA_HEADS // 2):
            lanes = slice(p * LANES, (p + 1) * LANES)
            es, invs = [], []
            for g in range(2 * GRID_W // NA_SOFTMAX_ROWS):
                rows = slice(g * NA_SOFTMAX_ROWS, (g + 1) * NA_SOFTMAX_ROWS)
                sc = scores[p][rows] + bias_ref[row - row_start, p, rows, :]
                m = jnp.max(sc, axis=-1, keepdims=True)
                e = jnp.exp2(sc - m)
                invs.append(1.0 / jnp.sum(e, axis=-1, keepdims=True))
                es.append(e.astype(BF16))
            inv = jnp.concatenate(invs, axis=0)
            o2 = _dot(jnp.concatenate(es, axis=0), v_ref[keys, lanes])
            o = jnp.where(low_p, o2[0:GRID_W] * inv[0:GRID_W], o2[GRID_W:] * inv[GRID_W:])
            out_ref[qrows, lanes] = (o * _silu(gate[qrows, lanes])).astype(BF16)


def _na_bias_table(rpb):
    c = jnp.arange(GRID_W, dtype=jnp.int32)
    cs = jnp.clip(c - NA_COLS // 2, 0, GRID_W - NA_COLS)
    cp = jnp.arange(GRID_W, dtype=jnp.int32)
    valid = (cp[None, :] >= cs[:, None]) & (cp[None, :] < cs[:, None] + NA_COLS)
    col_off = cp[None, :] - c[:, None] + NA_COLS - 1
    offs = jnp.arange(2 * NA_COLS - 1, dtype=jnp.int32)
    pick = ((col_off[None] == offs[:, None, None]) & valid[None]).astype(F32)
    t = jnp.einsum("hro,ocp->hcrp", rpb.astype(F32), pick, precision=lax.Precision.HIGHEST)
    t = jnp.where(valid[None, :, None, :], t * LOG2E, NEG_BIAS)
    tbl = jnp.stack([t[:, :, NA_ROWS - 1 - di:2 * NA_ROWS - 1 - di, :] for di in range(NA_ROWS)])
    return tbl.reshape(NA_ROWS, NA_HEADS // 2, 2 * GRID_W, NA_ROWS * GRID_W)


def _na_mixer(z3, bias_tbl, q_g, k_g, ones_bd, *, first_block):
    b, s, _ = z3.shape
    w = GROUP_W
    n_rows = s // GRID_W
    assert n_rows >= NA_ROWS and n_rows % NA_ROWS_PER_STEP == 0
    fb = first_block
    tq = NA_ROWS_PER_STEP * GRID_W
    const = lambda a: pl.BlockSpec(a.shape, lambda i, r, n=a.ndim: (0,) * n)
    return pl.pallas_call(
        functools.partial(_na_kernel, n_rows=n_rows, rows_per_step=NA_ROWS_PER_STEP),
        grid=(b, n_rows // NA_ROWS_PER_STEP),
        in_specs=[
            pl.BlockSpec((None, tq, w), lambda i, r: (i, r, fb)),
            pl.BlockSpec((None, s, w), lambda i, r: (i, 0, fb + 1)),
            pl.BlockSpec((None, s, w), lambda i, r: (i, 0, fb + 2)),
            pl.BlockSpec((None, tq, w), lambda i, r: (i, r, fb + 3)),
            const(bias_tbl), const(q_g), const(k_g), const(ones_bd),
        ],
        out_specs=pl.BlockSpec((None, tq, w), lambda i, r: (i, r, 0)),
        out_shape=jax.ShapeDtypeStruct((b, s, w), BF16),
        scratch_shapes=[pltpu.VMEM((s, w), BF16)],
        compiler_params=_cparams("parallel", "arbitrary"),
        name="na_mixer",
    )(z3, z3, z3, z3, bias_tbl, q_g, k_g, ones_bd)


def _gla_kernel(q_ref, k_ref, v_ref, gate_ref, lr_ref, a2f_ref, abf_ref, a2b_ref, abb_ref,
                tri_ref, bd_ref, og_ref, out_ref,
                bf_ref, bb_ref, of_ref, ob_ref, stf_ref, stb_ref):
    s = q_ref.shape[0]
    dk_all = q_ref.shape[1]
    dv_all = v_ref.shape[1]
    c = GLA_CHUNK
    n_chunks = s // c
    rows = tri_ref.shape[1]

    def decay(i, carry):
        sl = pl.ds(pl.multiple_of(i * rows, rows), rows)
        lr = lr_ref[sl, :]
        lr_hi = lr.astype(BF16)
        lr_lo = (lr - lr_hi.astype(F32)).astype(BF16)
        for a2_ref, ab_ref, b_ref, tri in ((a2f_ref, abf_ref, bf_ref, tri_ref[0]),
                                          (a2b_ref, abb_ref, bb_ref, tri_ref[1])):
            z = (_dot(lr_hi, a2_ref[0]) + _dot(lr_lo, a2_ref[0]) + _dot(lr_hi, a2_ref[1])) + ab_ref[...]
            g = (jnp.minimum(z, 0.0) - jnp.log1p(jnp.exp(-jnp.abs(z)))) * (1.0 / GLA_TAU)
            g_hi, g_mid, g_lo = _split3(g)
            b_ref[sl, :] = _dot(tri, g_hi) + _dot(tri, g_mid) + _dot(tri, g_lo)
        return carry
    lax.fori_loop(0, s // rows, decay, 0, unroll=2)

    stf_ref[...] = jnp.zeros_like(stf_ref)
    stb_ref[...] = jnp.zeros_like(stb_ref)

    lane = lax.broadcasted_iota(jnp.int32, (1, dk_all), 1)
    head_masks = [(lane // GLA_DK) == h for h in range(GLA_HEADS)]
    ri = lax.broadcasted_iota(jnp.int32, (c, c), 0)
    ci = lax.broadcasted_iota(jnp.int32, (c, c), 1)
    causal = (ri >= ci, ri <= ci)

    def one_chunk(ch, forward):
        b_ref, st_ref, o_ref = (bf_ref, stf_ref, of_ref) if forward else (bb_ref, stb_ref, ob_ref)
        sl = pl.ds(pl.multiple_of(ch * c, c), c)
        b = b_ref[sl, :]
        b_last = b[c - 1:c, :] if forward else b[0:1, :]
        q = q_ref[sl, :].astype(F32)
        k = k_ref[sl, :].astype(F32)
        v = v_ref[sl, :]
        qd = (q * ((GLA_DK ** -0.5) * jnp.exp(b))).astype(BF16)
        ki = (k * jnp.exp(-b)).astype(BF16)
        kd = (k * jnp.exp(b_last - b)).astype(BF16)
        zero = jnp.zeros_like(qd)
        q_stack = jnp.concatenate([jnp.where(hm, qd, zero) for hm in head_masks], axis=0)
        attn = _dot_nt(q_stack, ki)
        mask = causal[0] if forward else causal[1]
        state = st_ref[...]
        o = _dot_nt(qd, state.astype(BF16))
        intra = []
        for h in range(GLA_HEADS):
            a_h = jnp.where(mask, attn[h * c:(h + 1) * c, :], 0.0).astype(BF16)
            intra.append(_dot(a_h, v[:, h * GLA_DV:(h + 1) * GLA_DV]))
        o_ref[sl, :] = o + jnp.concatenate(intra, axis=-1)
        st_ref[...] = state * jnp.exp(b_last) + _dot_tn(v, kd) * bd_ref[...]

    def step(t, carry):
        one_chunk(t, True)
        one_chunk(n_chunks - 1 - t, False)
        return carry
    lax.fori_loop(0, n_chunks, step, 0, unroll=2)

    def finish(i, carry):
        sl = pl.ds(pl.multiple_of(i * rows, rows), rows)
        o = of_ref[sl, :] + ob_ref[sl, :]
        outs = []
        for h in range(GLA_HEADS):
            oh = o[:, h * GLA_DV:(h + 1) * GLA_DV]
            ms = jnp.mean(oh * oh, axis=-1, keepdims=True)
            outs.append(oh * lax.rsqrt(ms + EPS))
        y = jnp.concatenate(outs, axis=-1) * og_ref[...]
        gate = gate_ref[sl, :].astype(F32)
        out_ref[sl, :] = (y * _silu(gate)).astype(BF16)
        return carry
    lax.fori_loop(0, s // rows, finish, 0, unroll=2)


def _gla_mixer(z3, lr3, a2f, abf, a2b, abb, tri, bd_mask, o_g, *, q_block, v_block):
    b, s, _ = z3.shape
    dk_all = GLA_HEADS * GLA_DK
    dv_all = GLA_HEADS * GLA_DV
    full = lambda a: pl.BlockSpec(a.shape, lambda i, n=a.ndim: (0,) * n)
    return pl.pallas_call(
        _gla_kernel,
        grid=(b,),
        in_specs=[
            pl.BlockSpec((None, s, dk_all), lambda i: (i, 0, q_block)),
            pl.BlockSpec((None, s, dk_all), lambda i: (i, 0, q_block + 1)),
            pl.BlockSpec((None, s, dv_all), lambda i: (i, 0, v_block)),
            pl.BlockSpec((None, s, dv_all), lambda i: (i, 0, v_block + 1)),
            pl.BlockSpec((None, s, LANES), lambda i: (i, 0, 0)),
            full(a2f), full(abf), full(a2b), full(abb), full(tri), full(bd_mask), full(o_g),
        ],
        out_specs=pl.BlockSpec((None, s, dv_all), lambda i: (i, 0, 0)),
        out_shape=jax.ShapeDtypeStruct((b, s, dv_all), BF16),
        scratch_shapes=[
            pltpu.VMEM((s, dk_all), F32), pltpu.VMEM((s, dk_all), F32),
            pltpu.VMEM((s, dv_all), F32), pltpu.VMEM((s, dv_all), F32),
            pltpu.VMEM((dv_all, dk_all), F32), pltpu.VMEM((dv_all, dk_all), F32),
        ],
        compiler_params=_cparams("parallel"),
        name="gla_mixer",
    )(z3, z3, z3, z3, lr3, a2f, abf, a2b, abb, tri, bd_mask, o_g)


def _outproj_kernel(x_ref, ya_ref, yb_ref, yc_ref, yd_ref, w_ref, o_ref):
    y = jnp.concatenate([ya_ref[...], yb_ref[...], yc_ref[...], yd_ref[...]], axis=-1)
    o_ref[...] = x_ref[...] + _dot(y, w_ref[...])


def _outproj(x2d, ys, w_out, *, tm):
    m, d = x2d.shape
    gw = ys[0].shape[1]
    yspec = pl.BlockSpec((tm, gw), lambda i: (i, 0))
    return pl.pallas_call(
        _outproj_kernel,
        grid=(m // tm,),
        in_specs=[pl.BlockSpec((tm, d), lambda i: (i, 0)), yspec, yspec, yspec, yspec,
                  pl.BlockSpec(w_out.shape, lambda i: (0, 0))],
        out_specs=pl.BlockSpec((tm, d), lambda i: (i, 0)),
        out_shape=jax.ShapeDtypeStruct((m, d), F32),
        compiler_params=_cparams("parallel"),
        name="outproj",
    )(x2d, *ys, w_out)


def _gla_constants():
    rows = 4 * GLA_CHUNK
    i = jnp.arange(rows)
    same = (i[:, None] // GLA_CHUNK) == (i[None, :] // GLA_CHUNK)
    lower = same & (i[:, None] >= i[None, :])
    upper = same & (i[:, None] <= i[None, :])
    tri = jnp.stack([lower, upper]).astype(BF16)
    rv = jnp.arange(GLA_HEADS * GLA_DV) // GLA_DV
    rk = jnp.arange(GLA_HEADS * GLA_DK) // GLA_DK
    bd_mask = (rv[:, None] == rk[None, :]).astype(F32)
    return tri, bd_mask


def _pad_rank(a2, first_row):
    full = jnp.zeros((LANES, a2.shape[1]), F32).at[first_row:first_row + a2.shape[0]].set(a2.astype(F32))
    hi = full.astype(BF16)
    lo = (full - hi.astype(F32)).astype(BF16)
    return jnp.stack([hi, lo])


def _layer(x, norm_g, w_in, conv_w, conv_b, conv_ln_g, conv_ln_b, na_q_g, na_k_g, na_rpb,
           gla_a2_f, gla_ab_f, gla_a2_b, gla_ab_b, gla_o_g, pool_w, pool_scale, w_out, consts):
    b, s, d = x.shape
    m = b * s
    tri, bd_mask, ones_bd = consts
    lr0 = 7 * GROUP_W + 2 * GLA_HEADS * GLA_DK + 2 * GROUP_W
    lr1 = lr0 + 2 * GLA_RANK
    n_main = w_in.shape[1] - 2 * GLA_RANK
    w_all = jnp.concatenate([w_in[:, :lr0], w_in[:, lr1:], w_in[:, lr0:lr1],
                             jnp.zeros((d, LANES - 2 * GLA_RANK), w_in.dtype)], axis=1).astype(BF16)

    x2d = x.reshape(m, d)
    z, lr = _inproj(x2d, norm_g.reshape(1, d), w_all, w_all, n=n_main, lr_block=n_main // LANES, **INPROJ_TILES)
    z3 = z.reshape(b, s, z.shape[1])
    lr3 = lr.reshape(b, s, LANES)

    row = lambda a: a.reshape(1, -1).astype(F32)
    conv_w8 = jnp.repeat(conv_w.astype(F32), SUBLANES, axis=0)
    y_a = _conv_mixer(z3, conv_w8, row(conv_b), row(conv_ln_g), row(conv_ln_b))
    y_b = _na_mixer(z3, _na_bias_table(na_rpb), row(na_q_g), row(na_k_g), ones_bd, first_block=3)
    y_c = _gla_mixer(z3, lr3, _pad_rank(gla_a2_f, 0), row(gla_ab_f), _pad_rank(gla_a2_b, GLA_RANK),
                     row(gla_ab_b), tri, bd_mask, row(gla_o_g), q_block=14, v_block=8)
    y_d = _pool_mixer(z3, pool_w.astype(BF16), row(pool_scale), first_block=10)

    ys = [y.reshape(m, GROUP_W) for y in (y_a, y_b, y_c, y_d)]
    out = _outproj(x2d, ys, w_out.astype(BF16), **OUTPROJ_TILES)
    return out.reshape(b, s, d)


def kernel(x, norm_g, w_in, conv_w, conv_b, conv_ln_g, conv_ln_b, na_q_g, na_k_g, na_rpb, gla_a2_f,
           gla_ab_f, gla_a2_b, gla_ab_b, gla_o_g, pool_w, pool_scale, w_out):
    tri, bd_mask = _gla_constants()
    hq = jnp.arange(GROUP_W) // NA_HEAD_DIM
    ones_bd = (hq[:, None] == hq[None, :]).astype(BF16)
    consts = (tri, bd_mask, ones_bd)
    params = (norm_g, w_in, conv_w, conv_b, conv_ln_g, conv_ln_b, na_q_g, na_k_g, na_rpb, gla_a2_f,
              gla_ab_f, gla_a2_b, gla_ab_b, gla_o_g, pool_w, pool_scale, w_out)
    for l in range(norm_g.shape[0]):
        x = _layer(x, *[p[l] for p in params], consts)
    return x
```

```python
import functools

import jax
import jax.numpy as jnp
from jax import lax
from jax.experimental import pallas as pl
from jax.experimental.pallas import tpu as pltpu

F32 = jnp.float32
BF16 = jnp.bfloat16

EPS = 1e-6
GRID_W = 64
GROUP_W = 512
CONV_K = 31
NA_HEADS = 8
NA_HEAD_DIM = 64
NA_ROWS = 8
NA_COLS = 16
GLA_HEADS = 4
GLA_DK = 64
GLA_DV = 128
GLA_RANK = 16
GLA_TAU = 16.0
GLA_CHUNK = 64
POOL_WINDOWS = (2, 4, 8, 16)
POOL_CG = 128
LANES = 128
SUBLANES = 8
NEG_BIAS = -1e30
NA_ROWS_PER_STEP = 8
NA_SOFTMAX_ROWS = 32
LOG2E = 1.4426950408889634

VMEM_LIMIT = 56 * 1024 * 1024
INPROJ_TILES = dict(tm=1024, tn=2048)
INPROJ_NORM_ROWS = 256
OUTPROJ_TILES = dict(tm=512)


def _cparams(*sem):
    return pltpu.CompilerParams(dimension_semantics=sem, vmem_limit_bytes=VMEM_LIMIT)


def _silu(x):
    return x * jax.nn.sigmoid(x)


def _dot(a, b):
    return jnp.dot(a, b, preferred_element_type=F32)


def _dot_nt(a, b):
    return lax.dot_general(a, b, (((1,), (1,)), ((), ())), preferred_element_type=F32)


def _dot_tn(a, b):
    return lax.dot_general(a, b, (((0,), (0,)), ((), ())), preferred_element_type=F32)


def _split3(x):
    hi = x.astype(BF16)
    r1 = x - hi.astype(F32)
    mid = r1.astype(BF16)
    lo = (r1 - mid.astype(F32)).astype(BF16)
    return hi, mid, lo


def _inproj_kernel(x_ref, g_ref, w_ref, wlr_ref, z_ref, lr_ref, h_ref, *, rows_per_step):
    j = pl.program_id(1)
    tm = x_ref.shape[0]

    @pl.when(j == 0)
    def _():
        for c in range(tm // rows_per_step):
            r = slice(c * rows_per_step, (c + 1) * rows_per_step)
            x = x_ref[r, :]
            ms = jnp.mean(x * x, axis=-1, keepdims=True)
            h = (x * lax.rsqrt(ms + EPS) * g_ref[...]).astype(BF16)
            h_ref[r, :] = h
            lr_ref[r, :] = _dot(h, wlr_ref[...])
            z_ref[r, :] = _dot(h, w_ref[...]).astype(BF16)

    @pl.when(j > 0)
    def _():
        z_ref[...] = _dot(h_ref[...], w_ref[...]).astype(BF16)


def _inproj(x2d, norm_g, w_main, w_lr, *, tm, tn):
    m, d = x2d.shape
    n = w_main.shape[1]
    assert m % tm == 0 and n % tn == 0
    return pl.pallas_call(
        functools.partial(_inproj_kernel, rows_per_step=INPROJ_NORM_ROWS),
        grid=(m // tm, n // tn),
        in_specs=[
            pl.BlockSpec((tm, d), lambda i, j: (i, 0)),
            pl.BlockSpec((1, d), lambda i, j: (0, 0)),
            pl.BlockSpec((d, tn), lambda i, j: (0, j)),
            pl.BlockSpec((d, LANES), lambda i, j: (0, 0)),
        ],
        out_specs=[
            pl.BlockSpec((tm, tn), lambda i, j: (i, j)),
            pl.BlockSpec((tm, LANES), lambda i, j: (i, 0)),
        ],
        out_shape=[
            jax.ShapeDtypeStruct((m, n), BF16),
            jax.ShapeDtypeStruct((m, LANES), F32),
        ],
        scratch_shapes=[pltpu.VMEM((tm, d), BF16)],
        compiler_params=_cparams("parallel", "arbitrary"),
        name="inproj",
    )(x2d, norm_g, w_main, w_lr)


CONV_PAD = 16


def _conv_kernel(val_ref, glu_ref, gate_ref, cw_ref, cb_ref, lg_ref, lb_ref, out_ref, u_ref):
    s = val_ref.shape[0]
    w = val_ref.shape[1]
    fill_rows = 128
    tile = 32

    u_ref[0:CONV_PAD, :] = jnp.zeros((CONV_PAD, w), F32)
    u_ref[CONV_PAD + s:CONV_PAD + s + CONV_PAD, :] = jnp.zeros((CONV_PAD, w), F32)

    def fill(i, c):
        base = pl.multiple_of(i * fill_rows, fill_rows)
        v = val_ref[pl.ds(base, fill_rows), :].astype(F32)
        g = glu_ref[pl.ds(base, fill_rows), :].astype(F32)
        u_ref[pl.ds(base + CONV_PAD, fill_rows), :] = v * jax.nn.sigmoid(g)
        return c
    lax.fori_loop(0, s // fill_rows, fill, 0)

    win_rows = tile + 2 * CONV_PAD

    def conv(i, c):
        base = pl.multiple_of(i * tile, tile)
        acc = jnp.zeros((tile, w), F32)
        for r in range(SUBLANES):
            offs = [o for o in range(CONV_PAD - CONV_K // 2, CONV_PAD + CONV_K // 2 + 1) if o % SUBLANES == r]
            win = u_ref[pl.ds(base, win_rows), :]
            if r:
                win = pltpu.roll(win, win_rows - r, axis=0)
            for o in offs:
                k = o - (CONV_PAD - CONV_K // 2)
                wk = pltpu.repeat(cw_ref[k * SUBLANES:(k + 1) * SUBLANES, :], tile // SUBLANES, axis=0)
                acc = acc + win[o - r:o - r + tile, :] * wk
        y = acc + cb_ref[...]
        mu = jnp.mean(y, axis=-1, keepdims=True)
        d = y - mu
        var = jnp.mean(d * d, axis=-1, keepdims=True)
        yn = d * lax.rsqrt(var + EPS) * lg_ref[...] + lb_ref[...]
        gate = gate_ref[pl.ds(base, tile), :].astype(F32)
        out_ref[pl.ds(base, tile), :] = (_silu(yn) * _silu(gate)).astype(BF16)
        return c
    lax.fori_loop(0, s // tile, conv, 0, unroll=4)


def _conv_mixer(z3, conv_w, conv_b, ln_g, ln_b):
    b, s, _ = z3.shape
    w = GROUP_W
    col = lambda j: pl.BlockSpec((None, s, w), lambda i, j=j: (i, 0, j))
    vec = lambda r: pl.BlockSpec((r, w), lambda i: (0, 0))
    return pl.pallas_call(
        _conv_kernel,
        grid=(b,),
        in_specs=[col(0), col(1), col(2), vec(CONV_K * SUBLANES), vec(1), vec(1), vec(1)],
        out_specs=pl.BlockSpec((None, s, w), lambda i: (i, 0, 0)),
        out_shape=jax.ShapeDtypeStruct((b, s, w), BF16),
        scratch_shapes=[pltpu.VMEM((s + 2 * CONV_PAD, w), F32)],
        compiler_params=_cparams("parallel"),
        name="conv_mixer",
    )(z3, z3, z3, conv_w, conv_b, ln_g, ln_b)


POOL_PAD = 8


def _pool_kernel(val_ref, gate_ref, pw_ref, ps_ref, out_ref, u_ref):
    s = val_ref.shape[0]
    w = val_ref.shape[1]
    tile = 128

    u_ref[0:POOL_PAD, :] = jnp.zeros((POOL_PAD, w), F32)
    u_ref[POOL_PAD + s:POOL_PAD + s + POOL_PAD, :] = jnp.zeros((POOL_PAD, w), F32)

    def fill(i, c):
        base = pl.multiple_of(i * tile, tile)
        u_ref[pl.ds(base + POOL_PAD, tile), :] = val_ref[pl.ds(base, tile), :].astype(F32)
        return c
    lax.fori_loop(0, s // tile, fill, 0)

    win_rows = tile + 2 * POOL_PAD

    def shifted(a, o):
        return a if o == 0 else pltpu.roll(a, win_rows - o, axis=0)

    def pool(i, c):
        base = pl.multiple_of(i * tile, tile)
        t = base + lax.broadcasted_iota(jnp.int32, (tile, 1), 0)
        ys = []
        for g, win in enumerate(POOL_WINDOWS):
            half = win // 2
            lanes = slice(g * POOL_CG, (g + 1) * POOL_CG)
            x = u_ref[pl.ds(base, win_rows), lanes]
            sm, span = x, 1
            while span < win:
                sm = sm + shifted(sm, span)
                span *= 2
            acc = shifted(sm, POOL_PAD - half)[0:tile]
            cnt = jnp.minimum(t + half, s) - jnp.maximum(t - half, 0)
            centre = x[POOL_PAD:POOL_PAD + tile]
            diff = acc / cnt.astype(F32) - centre
            ys.append(_dot(diff.astype(BF16), pw_ref[g]))
        y = jnp.concatenate(ys, axis=-1) * ps_ref[...]
        gate = gate_ref[pl.ds(base, tile), :].astype(F32)
        out_ref[pl.ds(base, tile), :] = (y * _silu(gate)).astype(BF16)
        return c
    lax.fori_loop(0, s // tile, pool, 0, unroll=2)


def _pool_mixer(z3, pool_w, pool_scale, *, first_block):
    b, s, _ = z3.shape
    w = GROUP_W
    col = lambda j: pl.BlockSpec((None, s, w), lambda i, j=j: (i, 0, j))
    return pl.pallas_call(
        _pool_kernel,
        grid=(b,),
        in_specs=[
            col(first_block), col(first_block + 1),
            pl.BlockSpec(pool_w.shape, lambda i: (0, 0, 0)),
            pl.BlockSpec((1, w), lambda i: (0, 0)),
        ],
        out_specs=pl.BlockSpec((None, s, w), lambda i: (i, 0, 0)),
        out_shape=jax.ShapeDtypeStruct((b, s, w), BF16),
        scratch_shapes=[pltpu.VMEM((s + 2 * POOL_PAD, w), F32)],
        compiler_params=_cparams("parallel"),
        name="pool_mixer",
    )(z3, z3, pool_w, pool_scale)


def _head_rms(x, ones_bd, gain):
    ss = _dot((x * x).astype(BF16), ones_bd)
    return x * lax.rsqrt(ss * (1.0 / NA_HEAD_DIM) + EPS) * gain


def _na_kernel(q_ref, k_ref, v_ref, gate_ref, bias_ref, qg_ref, kg_ref, ones_ref, out_ref,
               kn_ref, *, n_rows, rows_per_step):
    r = pl.program_id(1)
    s = k_ref.shape[0]
    n_keys = NA_ROWS * GRID_W

    @pl.when(r == 0)
    def _():
        rows = 256

        def prep(i, c):
            sl = pl.ds(pl.multiple_of(i * rows, rows), rows)
            k = k_ref[sl, :].astype(F32)
            kn_ref[sl, :] = _head_rms(k, ones_ref[...], kg_ref[...]).astype(BF16)
            return c
        lax.fori_loop(0, s // rows, prep, 0)

    q = q_ref[...].astype(F32)
    qn = (_head_rms(q, ones_ref[...], qg_ref[...]) * (LOG2E * NA_HEAD_DIM ** -0.5)).astype(BF16)
    lane_p = lax.broadcasted_iota(jnp.int32, (1, LANES), 1)
    low_p = lane_p < NA_HEAD_DIM
    gate = gate_ref[...].astype(F32)
    ones_keys = jnp.ones((n_keys, LANES), BF16)

    for qi in range(rows_per_step):
        row = r * rows_per_step + qi
        row_start = jnp.clip(row - NA_ROWS // 2, 0, n_rows - NA_ROWS)
        keys = pl.ds(pl.multiple_of(row_start * GRID_W, GRID_W), n_keys)
        qrows = slice(qi * GRID_W, (qi + 1) * GRID_W)
        scores = []
        for p in range(NA_HEADS // 2):
            lanes = slice(p * LANES, (p + 1) * LANES)
            qp = qn[qrows, lanes]
            zero = jnp.zeros_like(qp)
            qm = jnp.concatenate([jnp.where(low_p, qp, zero), jnp.where(low_p, zero, qp)], axis=0)
            scores.append(_dot_nt(qm, kn_ref[keys, lanes]))
        for p in range(NA_HEADS // 2):
            lanes = slice(p * LANES, (p + 1) * LANES)
            es = []
            for g in range(2 * GRID_W // NA_SOFTMAX_ROWS):
                rows = slice(g * NA_SOFTMAX_ROWS, (g + 1) * NA_SOFTMAX_ROWS)
                sc = scores[p][rows] + bias_ref[row - row_start, p, rows, :]
                m = jnp.max(sc, axis=-1, keepdims=True)
                es.append(jnp.exp2(sc - m).astype(BF16))
            v_ext = jnp.concatenate([v_ref[keys, lanes], ones_keys], axis=-1)
            o2 = _dot(jnp.concatenate(es, axis=0), v_ext)
            o2 = o2[:, 0:LANES] / o2[:, LANES:]
            o = jnp.where(low_p, o2[0:GRID_W], o2[GRID_W:])
            out_ref[qrows, lanes] = (o * _silu(gate[qrows, lanes])).astype(BF16)


def _na_bias_table(rpb):
    c = jnp.arange(GRID_W, dtype=jnp.int32)
    cs = jnp.clip(c - NA_COLS // 2, 0, GRID_W - NA_COLS)
    cp = jnp.arange(GRID_W, dtype=jnp.int32)
    valid = (cp[None, :] >= cs[:, None]) & (cp[None, :] < cs[:, None] + NA_COLS)
    col_off = cp[None, :] - c[:, None] + NA_COLS - 1
    offs = jnp.arange(2 * NA_COLS - 1, dtype=jnp.int32)
    pick = ((col_off[None] == offs[:, None, None]) & valid[None]).astype(F32)
    t = jnp.einsum("hro,ocp->hcrp", rpb.astype(F32), pick, precision=lax.Precision.HIGHEST)
    t = jnp.where(valid[None, :, None, :], t * LOG2E, NEG_BIAS)
    tbl = jnp.stack([t[:, :, NA_ROWS - 1 - di:2 * NA_ROWS - 1 - di, :] for di in range(NA_ROWS)])
    return tbl.reshape(NA_ROWS, NA_HEADS // 2, 2 * GRID_W, NA_ROWS * GRID_W)


def _na_mixer(z3, bias_tbl, q_g, k_g, ones_bd, *, first_block):
    b, s, _ = z3.shape
    w = GROUP_W
    n_rows = s // GRID_W
    assert n_rows >= NA_ROWS and n_rows % NA_ROWS_PER_STEP == 0
    fb = first_block
    tq = NA_ROWS_PER_STEP * GRID_W
    const = lambda a: pl.BlockSpec(a.shape, lambda i, r, n=a.ndim: (0,) * n)
    return pl.pallas_call(
        functools.partial(_na_kernel, n_rows=n_rows, rows_per_step=NA_ROWS_PER_STEP),
        grid=(b, n_rows // NA_ROWS_PER_STEP),
        in_specs=[
            pl.BlockSpec((None, tq, w), lambda i, r: (i, r, fb)),
            pl.BlockSpec((None, s, w), lambda i, r: (i, 0, fb + 1)),
            pl.BlockSpec((None, s, w), lambda i, r: (i, 0, fb + 2)),
            pl.BlockSpec((None, tq, w), lambda i, r: (i, r, fb + 3)),
            const(bias_tbl), const(q_g), const(k_g), const(ones_bd),
        ],
        out_specs=pl.BlockSpec((None, tq, w), lambda i, r: (i, r, 0)),
        out_shape=jax.ShapeDtypeStruct((b, s, w), BF16),
        scratch_shapes=[pltpu.VMEM((s, w), BF16)],
        compiler_params=_cparams("parallel", "arbitrary"),
        name="na_mixer",
    )(z3, z3, z3, z3, bias_tbl, q_g, k_g, ones_bd)


def _gla_kernel(q_ref, k_ref, v_ref, gate_ref, lr_ref, a2f_ref, abf_ref, a2b_ref, abb_ref,
                tri_ref, bd_ref, og_ref, out_ref,
                bf_ref, bb_ref, of_ref, ob_ref, stf_ref, stb_ref):
    s = q_ref.shape[0]
    dk_all = q_ref.shape[1]
    dv_all = v_ref.shape[1]
    c = GLA_CHUNK
    n_chunks = s // c
    rows = tri_ref.shape[1]

    def decay(i, carry):
        sl = pl.ds(pl.multiple_of(i * rows, rows), rows)
        lr = lr_ref[sl, :]
        lr_hi = lr.astype(BF16)
        lr_lo = (lr - lr_hi.astype(F32)).astype(BF16)
        for a2_ref, ab_ref, b_ref, tri in ((a2f_ref, abf_ref, bf_ref, tri_ref[0]),
                                          (a2b_ref, abb_ref, bb_ref, tri_ref[1])):
            z = (_dot(lr_hi, a2_ref[0]) + _dot(lr_lo, a2_ref[0]) + _dot(lr_hi, a2_ref[1])) + ab_ref[...]
            g = (jnp.minimum(z, 0.0) - jnp.log1p(jnp.exp(-jnp.abs(z)))) * (1.0 / GLA_TAU)
            g_hi, g_mid, g_lo = _split3(g)
            b_ref[sl, :] = _dot(tri, g_hi) + _dot(tri, g_mid) + _dot(tri, g_lo)
        return carry
    lax.fori_loop(0, s // rows, decay, 0, unroll=2)

    stf_ref[...] = jnp.zeros_like(stf_ref)
    stb_ref[...] = jnp.zeros_like(stb_ref)

    lane = lax.broadcasted_iota(jnp.int32, (1, dk_all), 1)
    head_masks = [(lane // GLA_DK) == h for h in range(GLA_HEADS)]
    ri = lax.broadcasted_iota(jnp.int32, (c, c), 0)
    ci = lax.broadcasted_iota(jnp.int32, (c, c), 1)
    causal = (ri >= ci, ri <= ci)

    def one_chunk(ch, forward):
        b_ref, st_ref, o_ref = (bf_ref, stf_ref, of_ref) if forward else (bb_ref, stb_ref, ob_ref)
        sl = pl.ds(pl.multiple_of(ch * c, c), c)
        b = b_ref[sl, :]
        b_last = b[c - 1:c, :] if forward else b[0:1, :]
        q = q_ref[sl, :].astype(F32)
        k = k_ref[sl, :].astype(F32)
        v = v_ref[sl, :]
        qd = (q * ((GLA_DK ** -0.5) * jnp.exp(b))).astype(BF16)
        ki = (k * jnp.exp(-b)).astype(BF16)
        kd = (k * jnp.exp(b_last - b)).astype(BF16)
        zero = jnp.zeros_like(qd)
        q_stack = jnp.concatenate([jnp.where(hm, qd, zero) for hm in head_masks], axis=0)
        attn = _dot_nt(q_stack, ki)
        mask = causal[0] if forward else causal[1]
        state = st_ref[...]
        o = _dot_nt(qd, state.astype(BF16))
        intra = []
        for h in range(GLA_HEADS):
            a_h = jnp.where(mask, attn[h * c:(h + 1) * c, :], 0.0).astype(BF16)
            intra.append(_dot(a_h, v[:, h * GLA_DV:(h + 1) * GLA_DV]))
        o_ref[sl, :] = o + jnp.concatenate(intra, axis=-1)
        st_ref[...] = state * jnp.exp(b_last) + _dot_tn(v, kd) * bd_ref[...]

    def step(t, carry):
        one_chunk(t, True)
        one_chunk(n_chunks - 1 - t, False)
        return carry
    lax.fori_loop(0, n_chunks, step, 0, unroll=2)

    def finish(i, carry):
        sl = pl.ds(pl.multiple_of(i * rows, rows), rows)
        o = of_ref[sl, :] + ob_ref[sl, :]
        outs = []
        for h in range(GLA_HEADS):
            oh = o[:, h * GLA_DV:(h + 1) * GLA_DV]
            ms = jnp.mean(oh * oh, axis=-1, keepdims=True)
            outs.append(oh * lax.rsqrt(ms + EPS))
        y = jnp.concatenate(outs, axis=-1) * og_ref[...]
        gate = gate_ref[sl, :].astype(F32)
        out_ref[sl, :] = (y * _silu(gate)).astype(BF16)
        return carry
    lax.fori_loop(0, s // rows, finish, 0, unroll=2)


def _gla_mixer(z3, lr3, a2f, abf, a2b, abb, tri, bd_mask, o_g, *, q_block, v_block):
    b, s, _ = z3.shape
    dk_all = GLA_HEADS * GLA_DK
    dv_all = GLA_HEADS * GLA_DV
    full = lambda a: pl.BlockSpec(a.shape, lambda i, n=a.ndim: (0,) * n)
    return pl.pallas_call(
        _gla_kernel,
        grid=(b,),
        in_specs=[
            pl.BlockSpec((None, s, dk_all), lambda i: (i, 0, q_block)),
            pl.BlockSpec((None, s, dk_all), lambda i: (i, 0, q_block + 1)),
            pl.BlockSpec((None, s, dv_all), lambda i: (i, 0, v_block)),
            pl.BlockSpec((None, s, dv_all), lambda i: (i, 0, v_block + 1)),
            pl.BlockSpec((None, s, LANES), lambda i: (i, 0, 0)),
            full(a2f), full(abf), full(a2b), full(abb), full(tri), full(bd_mask), full(o_g),
        ],
        out_specs=pl.BlockSpec((None, s, dv_all), lambda i: (i, 0, 0)),
        out_shape=jax.ShapeDtypeStruct((b, s, dv_all), BF16),
        scratch_shapes=[
            pltpu.VMEM((s, dk_all), F32), pltpu.VMEM((s, dk_all), F32),
            pltpu.VMEM((s, dv_all), F32), pltpu.VMEM((s, dv_all), F32),
            pltpu.VMEM((dv_all, dk_all), F32), pltpu.VMEM((dv_all, dk_all), F32),
        ],
        compiler_params=_cparams("parallel"),
        name="gla_mixer",
    )(z3, z3, z3, z3, lr3, a2f, abf, a2b, abb, tri, bd_mask, o_g)


def _outproj_kernel(x_ref, ya_ref, yb_ref, yc_ref, yd_ref, w_ref, o_ref):
    y = jnp.concatenate([ya_ref[...], yb_ref[...], yc_ref[...], yd_ref[...]], axis=-1)
    o_ref[...] = x_ref[...] + _dot(y, w_ref[...])


def _outproj(x2d, ys, w_out, *, tm):
    m, d = x2d.shape
    gw = ys[0].shape[1]
    yspec = pl.BlockSpec((tm, gw), lambda i: (i, 0))
    return pl.pallas_call(
        _outproj_kernel,
        grid=(m // tm,),
        in_specs=[pl.BlockSpec((tm, d), lambda i: (i, 0)), yspec, yspec, yspec, yspec,
                  pl.BlockSpec(w_out.shape, lambda i: (0, 0))],
        out_specs=pl.BlockSpec((tm, d), lambda i: (i, 0)),
        out_shape=jax.ShapeDtypeStruct((m, d), F32),
        compiler_params=_cparams("parallel"),
        name="outproj",
    )(x2d, *ys, w_out)


def _gla_constants():
    rows = 4 * GLA_CHUNK
    i = jnp.arange(rows)
    same = (i[:, None] // GLA_CHUNK) == (i[None, :] // GLA_CHUNK)
    lower = same & (i[:, None] >= i[None, :])
    upper = same & (i[:, None] <= i[None, :])
    tri = jnp.stack([lower, upper]).astype(BF16)
    rv = jnp.arange(GLA_HEADS * GLA_DV) // GLA_DV
    rk = jnp.arange(GLA_HEADS * GLA_DK) // GLA_DK
    bd_mask = (rv[:, None] == rk[None, :]).astype(F32)
    return tri, bd_mask


def _pad_rank(a2, first_row):
    full = jnp.zeros((LANES, a2.shape[1]), F32).at[first_row:first_row + a2.shape[0]].set(a2.astype(F32))
    hi = full.astype(BF16)
    lo = (full - hi.astype(F32)).astype(BF16)
    return jnp.stack([hi, lo])


def _layer(x, norm_g, w_in, conv_w, conv_b, conv_ln_g, conv_ln_b, na_q_g, na_k_g, na_rpb,
           gla_a2_f, gla_ab_f, gla_a2_b, gla_ab_b, gla_o_g, pool_w, pool_scale, w_out, consts):
    b, s, d = x.shape
    m = b * s
    tri, bd_mask, ones_bd = consts
    lr0 = 7 * GROUP_W + 2 * GLA_HEADS * GLA_DK + 2 * GROUP_W
    lr1 = lr0 + 2 * GLA_RANK
    w_main = jnp.concatenate([w_in[:, :lr0], w_in[:, lr1:]], axis=1).astype(BF16)
    w_lr = jnp.zeros((d, LANES), F32).at[:, :2 * GLA_RANK].set(w_in[:, lr0:lr1]).astype(BF16)

    x2d = x.reshape(m, d)
    z, lr = _inproj(x2d, norm_g.reshape(1, d), w_main, w_lr, **INPROJ_TILES)
    z3 = z.reshape(b, s, z.shape[1])
    lr3 = lr.reshape(b, s, LANES)

    row = lambda a: a.reshape(1, -1).astype(F32)
    conv_w8 = jnp.repeat(conv_w.astype(F32), SUBLANES, axis=0)
    y_a = _conv_mixer(z3, conv_w8, row(conv_b), row(conv_ln_g), row(conv_ln_b))
    y_b = _na_mixer(z3, _na_bias_table(na_rpb), row(na_q_g), row(na_k_g), ones_bd, first_block=3)
    y_c = _gla_mixer(z3, lr3, _pad_rank(gla_a2_f, 0), row(gla_ab_f), _pad_rank(gla_a2_b, GLA_RANK),
                     row(gla_ab_b), tri, bd_mask, row(gla_o_g), q_block=14, v_block=8)
    y_d = _pool_mixer(z3, pool_w.astype(BF16), row(pool_scale), first_block=10)

    ys = [y.reshape(m, GROUP_W) for y in (y_a, y_b, y_c, y_d)]
    out = _outproj(x2d, ys, w_out.astype(BF16), **OUTPROJ_TILES)
    return out.reshape(b, s, d)


def kernel(x, norm_g, w_in, conv_w, conv_b, conv_ln_g, conv_ln_b, na_q_g, na_k_g, na_rpb, gla_a2_f,
           gla_ab_f, gla_a2_b, gla_ab_b, gla_o_g, pool_w, pool_scale, w_out):
    tri, bd_mask = _gla_constants()
    hq = jnp.arange(GROUP_W) // NA_HEAD_DIM
    ones_bd = (hq[:, None] == hq[None, :]).astype(BF16)
    consts = (tri, bd_mask, ones_bd)
    params = (norm_g, w_in, conv_w, conv_b, conv_ln_g, conv_ln_b, na_q_g, na_k_g, na_rpb, gla_a2_f,
              gla_ab_f, gla_a2_b, gla_ab_b, gla_o_g, pool_w, pool_scale, w_out)
    for l in range(norm_g.shape[0]):
        x = _layer(x, *[p[l] for p in params], consts)
    return x
```
